```python
import jax
import jax.numpy as jnp
from jax import lax
import numpy as np

D_MODEL = 1024
BATCH = 32
SEQ = 2048
DEPTH = 2

CHUNK = 64
Q_BLOCK = 128
MEM_LEN = 256

MLA_HEADS = 8
QK_NOPE_DIM = 64
QK_ROPE_DIM = 32
V_HEAD_DIM = 64
Q_LORA_RANK = 256
KV_LORA_RANK = 256
MLA_WIDTH = MLA_HEADS * V_HEAD_DIM
CONV_WIDTH = D_MODEL - MLA_WIDTH
CONV_TAPS = 31
IN_COLS = Q_LORA_RANK + KV_LORA_RANK + QK_ROPE_DIM + 2 * CONV_WIDTH

XA_HEADS = 4
XA_HEAD_DIM = D_MODEL // XA_HEADS

FFN_HIDDEN = ((8 * D_MODEL + 3 * 256 - 1) // (3 * 256)) * 256

ALPHA = (2.0 * DEPTH) ** 0.25
BETA = (8.0 * DEPTH) ** -0.25

ROPE_BASE = 10000.0
LN_EPS = 1e-5
RMS_EPS = 1e-6

kernel_name = "hybrid_mla_conformer_deepnorm_encoder"


def layer_norm(x, g, b):
    xf = x.astype(jnp.float32)
    mu = jnp.mean(xf, axis=-1, keepdims=True)
    var = jnp.mean(jnp.square(xf - mu), axis=-1, keepdims=True)
    y = (xf - mu) * lax.rsqrt(var + LN_EPS)
    return (y * g.astype(jnp.float32) + b.astype(jnp.float32)).astype(x.dtype)


def rms_norm(x, g):
    xf = x.astype(jnp.float32)
    y = xf * lax.rsqrt(jnp.mean(jnp.square(xf), axis=-1, keepdims=True) + RMS_EPS)
    return (y * g.astype(jnp.float32)).astype(x.dtype)


def rope_cos_sin(positions, dim):
    inv_freq = ROPE_BASE ** (-jnp.arange(0, dim, 2, dtype=jnp.float32) / dim)
    ang = positions.astype(jnp.float32)[..., None] * inv_freq
    return jnp.cos(ang), jnp.sin(ang)


def apply_rope(x, cos, sin):
    half = x.shape[-1] // 2
    x1 = x[..., :half].astype(jnp.float32)
    x2 = x[..., half:].astype(jnp.float32)
    return jnp.concatenate([x1 * cos - x2 * sin, x1 * sin + x2 * cos], axis=-1).astype(x.dtype)


def mla_group(c_q, c_kv, k_r, cos, sin, q_norm_g, w_uq, kv_norm_g, w_ukv):
    B, S, _ = c_q.shape
    q = (rms_norm(c_q, q_norm_g) @ w_uq).reshape(B, S, MLA_HEADS, QK_NOPE_DIM + QK_ROPE_DIM)
    q_nope, q_rope = q[..., :QK_NOPE_DIM], q[..., QK_NOPE_DIM:]
    q_rope = apply_rope(q_rope, cos[:, :, None, :], sin[:, :, None, :])
    kv = (rms_norm(c_kv, kv_norm_g) @ w_ukv).reshape(B, S, MLA_HEADS, QK_NOPE_DIM + V_HEAD_DIM)
    k_nope, v = kv[..., :QK_NOPE_DIM], kv[..., QK_NOPE_DIM:]
    k_rope = apply_rope(k_r, cos, sin)
    scale = (QK_NOPE_DIM + QK_ROPE_DIM) ** -0.5
    outs = []
    for start in range(0, S, Q_BLOCK):
        end = start + Q_BLOCK
        s = (jnp.einsum('bqhd,bkhd->bhqk', q_nope[:, start:end], k_nope[:, :end])
             + jnp.einsum('bqhr,bkr->bhqk', q_rope[:, start:end], k_rope[:, :end]))
        q_chunk = jnp.arange(start, end) // CHUNK
        k_chunk = jnp.arange(end) // CHUNK
        mask = k_chunk[None, :] <= q_chunk[:, None]
        s = jnp.where(mask[None, None], s.astype(jnp.float32) * scale, -jnp.inf)
        p = jax.nn.softmax(s, axis=-1).astype(v.dtype)
        outs.append(jnp.einsum('bhqk,bkhd->bqhd', p, v[:, :end]))
    o = jnp.concatenate(outs, axis=1)
    return o.reshape(B, S, MLA_WIDTH)


def conformer_conv_group(u, dw_w, dw_b, norm_g, norm_b):
    a, g = jnp.split(u, 2, axis=-1)
    h = a * jax.nn.sigmoid(g)
    h = jnp.pad(h, ((0, 0), (CONV_TAPS - 1, 0), (0, 0)))
    h = lax.conv_general_dilated(
        h, dw_w[:, None, :], window_strides=(1,), padding='VALID',
        dimension_numbers=('NWC', 'WIO', 'NWC'), feature_group_count=CONV_WIDTH) + dw_b
    h = layer_norm(h, norm_g, norm_b)
    return jax.nn.silu(h)


def hybrid_mixer(x, cos, sin, w_in, b_in, q_norm_g, w_uq, kv_norm_g, w_ukv,
                 dw_w, dw_b, cn_g, cn_b, w_o):
    proj = x @ w_in + b_in
    o1 = Q_LORA_RANK
    o2 = o1 + KV_LORA_RANK
    o3 = o2 + QK_ROPE_DIM
    c_q, c_kv, k_r, u = proj[..., :o1], proj[..., o1:o2], proj[..., o2:o3], proj[..., o3:]
    y_att = mla_group(c_q, c_kv, k_r, cos, sin, q_norm_g, w_uq, kv_norm_g, w_ukv)
    y_conv = conformer_conv_group(u, dw_w, dw_b, cn_g, cn_b)
    return jnp.concatenate([y_att, y_conv], axis=-1) @ w_o


def memory_cross_attention(x, mem, w_q, w_kv, w_o):
    B, S, D = x.shape
    M = mem.shape[1]
    q = (x @ w_q).reshape(B, S, XA_HEADS, XA_HEAD_DIM)
    kv = (mem @ w_kv).reshape(B, M, 2, XA_HEADS, XA_HEAD_DIM)
    k, v = kv[:, :, 0], kv[:, :, 1]
    s = jnp.einsum('bshd,bmhd->bhsm', q, k).astype(jnp.float32) * (XA_HEAD_DIM ** -0.5)
    p = jax.nn.softmax(s, axis=-1).astype(v.dtype)
    o = jnp.einsum('bhsm,bmhd->bshd', p, v).reshape(B, S, D)
    return o @ w_o


def swiglu_ffn(x, w_in, w_down):
    gu = x @ w_in
    g, up = gu[..., :FFN_HIDDEN], gu[..., FFN_HIDDEN:]
    return (jax.nn.silu(g) * up) @ w_down


def setup_inputs(seed: int = 0) -> dict:
    key = jax.random.key(seed)
    ks = jax.random.split(key, 32)
    L, D = DEPTH, D_MODEL

    def normal(k, shape, fan_in, scale=1.0):
        return jax.random.normal(k, shape, jnp.float32) * (scale * fan_in ** -0.5)

    def gain(k, shape):
        return 1.0 + 0.02 * jax.random.normal(k, shape, jnp.float32)

    def small(k, shape):
        return 0.02 * jax.random.normal(k, shape, jnp.float32)

    x = jax.random.normal(ks[0], (BATCH, SEQ, D), jnp.float32)
    mem = jax.random.normal(ks[1], (BATCH, MEM_LEN, D), jnp.float32)
    offset = jax.random.randint(ks[2], (BATCH, 1), 0, 4096, dtype=jnp.int32)
    positions = offset + jnp.arange(SEQ, dtype=jnp.int32)[None, :]

    w_uk = normal(ks[7], (L, KV_LORA_RANK, MLA_HEADS, QK_NOPE_DIM), KV_LORA_RANK)
    w_uv = normal(ks[8], (L, KV_LORA_RANK, MLA_HEADS, V_HEAD_DIM), KV_LORA_RANK, BETA)
    mla_w_ukv = jnp.concatenate([w_uk, w_uv], axis=-1).reshape(
        L, KV_LORA_RANK, MLA_HEADS * (QK_NOPE_DIM + V_HEAD_DIM))
    xa_w_k = normal(ks[18], (L, D, D), D)
    xa_w_v = normal(ks[19], (L, D, D), D, BETA)
    xa_w_kv = jnp.concatenate([xa_w_k, xa_w_v], axis=-1)

    return {
        "x": x,
        "mem": mem,
        "positions": positions,
        "mix_w_in": normal(ks[3], (L, D, IN_COLS), D),
        "mix_b_in": small(ks[4], (L, IN_COLS)),
        "mla_q_norm": gain(ks[5], (L, Q_LORA_RANK)),
        "mla_w_uq": normal(ks[6], (L, Q_LORA_RANK, MLA_HEADS * (QK_NOPE_DIM + QK_ROPE_DIM)), Q_LORA_RANK),
        "mla_kv_norm": gain(ks[9], (L, KV_LORA_RANK)),
        "mla_w_ukv": mla_w_ukv,
        "conv_dw_w": normal(ks[10], (L, CONV_TAPS, CONV_WIDTH), CONV_TAPS),
        "conv_dw_b": small(ks[11], (L, CONV_WIDTH)),
        "conv_norm_g": gain(ks[12], (L, CONV_WIDTH)),
        "conv_norm_b": small(ks[13], (L, CONV_WIDTH)),
        "mix_w_o": normal(ks[14], (L, D, D), D, BETA),
        "ln1_g": gain(ks[15], (L, D)),
        "ln1_b": small(ks[16], (L, D)),
        "xa_w_q": normal(ks[17], (L, D, D), D),
        "xa_w_kv": xa_w_kv,
        "xa_w_o": normal(ks[20], (L, D, D), D, BETA),
        "ln2_g": gain(ks[21], (L, D)),
        "ln2_b": small(ks[22], (L, D)),
        "ffn_w_in": normal(ks[23], (L, D, 2 * FFN_HIDDEN), D),
        "ffn_w_down": normal(ks[24], (L, FFN_HIDDEN, D), FFN_HIDDEN, BETA),
        "ln3_g": gain(ks[25], (L, D)),
        "ln3_b": small(ks[26], (L, D)),
    }


def reference(x, mem, positions, mix_w_in, mix_b_in, mla_q_norm, mla_w_uq, mla_kv_norm,
              mla_w_ukv, conv_dw_w, conv_dw_b, conv_norm_g, conv_norm_b, mix_w_o,
              ln1_g, ln1_b, xa_w_q, xa_w_kv, xa_w_o, ln2_g, ln2_b,
              ffn_w_in, ffn_w_down, ln3_g, ln3_b):
    cos, sin = rope_cos_sin(positions, QK_ROPE_DIM)
    for l in range(DEPTH):
        y = hybrid_mixer(x, cos, sin, mix_w_in[l], mix_b_in[l], mla_q_norm[l], mla_w_uq[l],
                         mla_kv_norm[l], mla_w_ukv[l], conv_dw_w[l], conv_dw_b[l],
                         conv_norm_g[l], conv_norm_b[l], mix_w_o[l])
        x = layer_norm(ALPHA * x + y, ln1_g[l], ln1_b[l])
        y = memory_cross_attention(x, mem, xa_w_q[l], xa_w_kv[l], xa_w_o[l])
        x = layer_norm(ALPHA * x + y, ln2_g[l], ln2_b[l])
        y = swiglu_ffn(x, ffn_w_in[l], ffn_w_down[l])
        x = layer_norm(ALPHA * x + y, ln3_g[l], ln3_b[l])
    return x
```

```python
import functools

import jax
import jax.numpy as jnp
import numpy as np
from jax import lax
from jax.experimental import pallas as pl
from jax.experimental.pallas import tpu as pltpu

F32 = jnp.float32
BF16 = jnp.bfloat16

D_MODEL = 1024
DEPTH = 2
CHUNK = 64
MLA_HEADS = 8
QK_NOPE_DIM = 64
QK_ROPE_DIM = 32
V_HEAD_DIM = 64
Q_LORA_RANK = 256
KV_LORA_RANK = 256
MLA_WIDTH = MLA_HEADS * V_HEAD_DIM
CONV_WIDTH = D_MODEL - MLA_WIDTH
CONV_TAPS = 31
XA_HEADS = 4
XA_HEAD_DIM = D_MODEL // XA_HEADS
FFN_HIDDEN = 2816
ALPHA = (2.0 * DEPTH) ** 0.25
ROPE_BASE = 10000.0
LN_EPS = 1e-5
RMS_EPS = 1e-6

LANES = 128
SUBLANES = 8
VMEM_LIMIT_BYTES = 56 * 1024 * 1024

HEAD_PAD = LANES
HALF_ROPE = QK_ROPE_DIM // 2
ROPE_LO = QK_NOPE_DIM
QK_PAD = MLA_HEADS * HEAD_PAD
SUM_LANE = V_HEAD_DIM

ROW_TILE = 512
Q_TILE = 256
KV_TILE = 256
CONV_HALO = 32
CONV_ROWS = 32
FFN_CHUNKS = ((0, 1024), (1024, 1024), (2048, 768))

IN_EXT = Q_LORA_RANK + KV_LORA_RANK + 2 * HEAD_PAD + 2 * CONV_WIDTH
MASK_VALUE = -1e30


def _dot(a, b):
    return jnp.dot(a, b, preferred_element_type=F32)


def _dot_nt(a, b):
    return lax.dot_general(a, b, (((1,), (1,)), ((), ())), preferred_element_type=F32)


def _rms_norm(x, g):
    return x * lax.rsqrt(jnp.mean(x * x, axis=-1, keepdims=True) + RMS_EPS) * g


def _layer_norm(x, g, b):
    mu = jnp.mean(x, axis=-1, keepdims=True)
    xc = x - mu
    var = jnp.mean(xc * xc, axis=-1, keepdims=True)
    return xc * lax.rsqrt(var + LN_EPS) * g + b


def _sigmoid(x):
    return 1.0 / (1.0 + jnp.exp(-x))


def _const_spec(shape):
    return pl.BlockSpec(shape, lambda *_: (0,) * len(shape))


def _params(semantics):
    return pltpu.CompilerParams(dimension_semantics=semantics, vmem_limit_bytes=VMEM_LIMIT_BYTES)


def _mix_in_kernel(tiles_per_seq, x_ref, pos_ref, freq_ref, w_in_ref, b_in_ref, qg_ref, kvg_ref,
                   wqa_ref, wqb_ref, wk_ref, wv_ref, vone_ref, dww_ref, dwb_ref, cng_ref, cnb_ref,
                   q_ref, k_ref, v_ref, yc_ref, hbuf_ref):
    rows = x_ref.shape[0]

    @pl.when(pl.program_id(0) % tiles_per_seq == 0)
    def _():
        hbuf_ref[0:CONV_HALO, :] = jnp.zeros((CONV_HALO, CONV_WIDTH), F32)

    proj = _dot(x_ref[...].astype(BF16), w_in_ref[...]) + b_in_ref[...]
    o_kv = Q_LORA_RANK
    o_kr = o_kv + KV_LORA_RANK
    o_u = o_kr + 2 * HEAD_PAD
    c_q = proj[:, 0:o_kv]
    c_kv = proj[:, o_kv:o_kr]
    kr_a = proj[:, o_kr:o_kr + HEAD_PAD]
    kr_b = proj[:, o_kr + HEAD_PAD:o_u]
    u_a = proj[:, o_u:o_u + CONV_WIDTH]
    u_g = proj[:, o_u + CONV_WIDTH:o_u + 2 * CONV_WIDTH]

    ang = pos_ref[...].astype(F32) * freq_ref[...]
    cos = jnp.cos(ang)
    sin = jnp.sin(ang)

    scale = (QK_NOPE_DIM + QK_ROPE_DIM) ** -0.5
    cqn = _rms_norm(c_q, qg_ref[...]).astype(BF16)
    q_a = _dot(cqn, wqa_ref[...])
    q_b = _dot(cqn, wqb_ref[...])
    ckn = _rms_norm(c_kv, kvg_ref[...]).astype(BF16)
    k_n = _dot(ckn, wk_ref[...])
    k_r = kr_a * cos + kr_b * sin
    for h in range(MLA_HEADS):
        sl = slice(h * HEAD_PAD, (h + 1) * HEAD_PAD)
        q_ref[:, sl] = ((q_a[:, sl] * cos + q_b[:, sl] * sin) * scale).astype(BF16)
        k_ref[:, sl] = (k_n[:, sl] + k_r).astype(BF16)
    v_ref[...] = (_dot(ckn, wv_ref[...]) + vone_ref[...]).astype(BF16)

    hbuf_ref[CONV_HALO:CONV_HALO + rows, :] = u_a * _sigmoid(u_g)
    base = CONV_HALO - (CONV_TAPS - 1)
    for r0 in range(0, rows, CONV_ROWS):
        acc = jnp.zeros((CONV_ROWS, CONV_WIDTH), F32) + dwb_ref[...]
        for t in range(CONV_TAPS):
            acc = acc + hbuf_ref[r0 + base + t:r0 + base + t + CONV_ROWS, :] * dww_ref[t:t + 1, :]
        y = _layer_norm(acc, cng_ref[...], cnb_ref[...])
        yc_ref[r0:r0 + CONV_ROWS, :] = (y * _sigmoid(y)).astype(BF16)
    hbuf_ref[0:CONV_HALO, :] = hbuf_ref[rows:rows + CONV_HALO, :]


def _mix_in(x2d, pos2d, freq, w_in, b_in, qg, kvg, wqa, wqb, wk, wv, vone, dww, dwb, cng, cnb, seq):
    t = x2d.shape[0]
    tm = ROW_TILE
    row = lambda w: pl.BlockSpec((tm, w), lambda i: (i, 0))
    consts = (freq, w_in, b_in, qg, kvg, wqa, wqb, wk, wv, vone, dww, dwb, cng, cnb)
    return pl.pallas_call(
        functools.partial(_mix_in_kernel, seq // tm),
        grid=(t // tm,),
        in_specs=[row(D_MODEL), row(1)] + [_const_spec(c.shape) for c in consts],
        out_specs=[row(QK_PAD), row(QK_PAD), row(QK_PAD), row(CONV_WIDTH)],
        out_shape=[jax.ShapeDtypeStruct((t, QK_PAD), BF16), jax.ShapeDtypeStruct((t, QK_PAD), BF16),
                   jax.ShapeDtypeStruct((t, QK_PAD), BF16), jax.ShapeDtypeStruct((t, CONV_WIDTH), BF16)],
        scratch_shapes=[pltpu.VMEM((tm + CONV_HALO, CONV_WIDTH), F32)],
        compiler_params=_params(("arbitrary",)),
        name="mix_in",
    )(x2d, pos2d, *consts)


def _mla_attn_kernel(q_ref, k_ref, v_ref, o_ref):
    qi = pl.program_id(1)
    tq = q_ref.shape[0]
    r_chunk = lax.broadcasted_iota(jnp.int32, (tq, KV_TILE), 0) // CHUNK
    c_chunk = lax.broadcasted_iota(jnp.int32, (tq, KV_TILE), 1) // CHUNK
    diag_mask = c_chunk <= r_chunk
    lane = lax.broadcasted_iota(jnp.int32, (tq, HEAD_PAD), 1)

    def head_out(h):
        sl = slice(h * HEAD_PAD, (h + 1) * HEAD_PAD)
        q = q_ref[:, sl]

        def step(j, carry, masked):
            m, acc = carry
            off = pl.multiple_of(j * KV_TILE, KV_TILE)
            s = _dot_nt(q, k_ref[pl.ds(off, KV_TILE), sl])
            if masked:
                s = jnp.where(diag_mask, s, MASK_VALUE)
            m_new = jnp.maximum(m, jnp.max(s, axis=1, keepdims=True))
            p = jnp.exp(s - m_new)
            acc = jnp.exp(m - m_new) * acc + _dot(p.astype(BF16), v_ref[pl.ds(off, KV_TILE), sl])
            return m_new, acc

        carry = (jnp.full((tq, 1), MASK_VALUE, F32), jnp.zeros((tq, HEAD_PAD), F32))
        carry = lax.fori_loop(0, qi, functools.partial(step, masked=False), carry)
        _, acc = step(qi, carry, masked=True)
        return acc / acc[:, SUM_LANE:SUM_LANE + 1]

    for hp in range(MLA_HEADS // 2):
        even = head_out(2 * hp)
        odd = head_out(2 * hp + 1)
        pair = jnp.where(lane < V_HEAD_DIM, even, pltpu.roll(odd, V_HEAD_DIM, 1))
        o_ref[:, hp * LANES:(hp + 1) * LANES] = pair.astype(BF16)


def _mla_attn(q, k, v, batch, seq):
    assert Q_TILE == KV_TILE and Q_TILE % CHUNK == 0
    return pl.pallas_call(
        _mla_attn_kernel,
        grid=(batch, seq // Q_TILE),
        in_specs=[pl.BlockSpec((Q_TILE, QK_PAD), lambda b, i: (b * (seq // Q_TILE) + i, 0)),
                  pl.BlockSpec((seq, QK_PAD), lambda b, i: (b, 0)),
                  pl.BlockSpec((seq, QK_PAD), lambda b, i: (b, 0))],
        out_specs=pl.BlockSpec((Q_TILE, MLA_WIDTH), lambda b, i: (b * (seq // Q_TILE) + i, 0)),
        out_shape=jax.ShapeDtypeStruct((batch * seq, MLA_WIDTH), BF16),
        compiler_params=_params(("arbitrary", "arbitrary")),
        name="mla_attn",
    )(q, k, v)


def _mix_out_kernel(o_ref, yc_ref, x_ref, woa_ref, woc_ref, g_ref, b_ref, wq_ref, x1_ref, qx_ref):
    y = _dot(o_ref[...], woa_ref[...]) + _dot(yc_ref[...], woc_ref[...])
    x1 = _layer_norm(ALPHA * x_ref[...] + y, g_ref[...], b_ref[...])
    x1_ref[...] = x1
    qx_ref[...] = (_dot(x1.astype(BF16), wq_ref[...]) * (XA_HEAD_DIM ** -0.5)).astype(BF16)


def _mix_out(o, yc, x2d, woa, woc, g, b, wq):
    t = x2d.shape[0]
    tm = ROW_TILE
    row = lambda w: pl.BlockSpec((tm, w), lambda i: (i, 0))
    consts = (woa, woc, g, b, wq)
    return pl.pallas_call(
        _mix_out_kernel,
        grid=(t // tm,),
        in_specs=[row(MLA_WIDTH), row(CONV_WIDTH), row(D_MODEL)] + [_const_spec(c.shape) for c in consts],
        out_specs=[row(D_MODEL), row(D_MODEL)],
        out_shape=[jax.ShapeDtypeStruct((t, D_MODEL), F32), jax.ShapeDtypeStruct((t, D_MODEL), BF16)],
        compiler_params=_params(("arbitrary",)),
        name="mix_out",
    )(o, yc, x2d, *consts)


def _xa_kv_kernel(mem_ref, wk_ref, wv_ref, k_ref, v_ref):
    m = mem_ref[...].astype(BF16)
    k_ref[...] = _dot(m, wk_ref[...]).astype(BF16)
    v_ref[...] = _dot(m, wv_ref[...]).astype(BF16)


def _xa_kv(mem2d, wk, wv):
    t = mem2d.shape[0]
    tm = ROW_TILE
    row = pl.BlockSpec((tm, D_MODEL), lambda i: (i, 0))
    return pl.pallas_call(
        _xa_kv_kernel,
        grid=(t // tm,),
        in_specs=[row, _const_spec(wk.shape), _const_spec(wv.shape)],
        out_specs=[row, row],
        out_shape=[jax.ShapeDtypeStruct((t, D_MODEL), BF16)] * 2,
        compiler_params=_params(("arbitrary",)),
        name="xa_kv",
    )(mem2d, wk, wv)


def _xattn_kernel(qx_ref, kx_ref, vx_ref, x1_ref, wo_ref, g_ref, b_ref, x2_ref):
    y = None
    for h in range(XA_HEADS):
        sl = slice(h * XA_HEAD_DIM, (h + 1) * XA_HEAD_DIM)
        s = _dot_nt(qx_ref[:, sl], kx_ref[:, sl])
        p = jnp.exp(s - jnp.max(s, axis=1, keepdims=True))
        l = jnp.sum(p, axis=1, keepdims=True)
        o = _dot(p.astype(BF16), vx_ref[:, sl]) / l
        part = _dot(o.astype(BF16), wo_ref[sl, :])
        y = part if y is None else y + part
    x2_ref[...] = _layer_norm(ALPHA * x1_ref[...] + y, g_ref[...], b_ref[...])


def _xattn(qx, kx, vx, x1, wo, g, b, seq, mem_len):
    t = x1.shape[0]
    tm = ROW_TILE
    row = pl.BlockSpec((tm, D_MODEL), lambda i: (i, 0))
    mem = pl.BlockSpec((mem_len, D_MODEL), lambda i: (i // (seq // tm), 0))
    consts = (wo, g, b)
    return pl.pallas_call(
        _xattn_kernel,
        grid=(t // tm,),
        in_specs=[row, mem, mem, row] + [_const_spec(c.shape) for c in consts],
        out_specs=row,
        out_shape=jax.ShapeDtypeStruct((t, D_MODEL), F32),
        compiler_params=_params(("arbitrary",)),
        name="xattn",
    )(qx, kx, vx, x1, *consts)


def _ffn_kernel(x_ref, wg_ref, wu_ref, wd_ref, g_ref, b_ref, o_ref):
    x = x_ref[...]
    xb = x.astype(BF16)
    y = None
    for c0, cw in FFN_CHUNKS:
        gate = _dot(xb, wg_ref[:, c0:c0 + cw])
        up = _dot(xb, wu_ref[:, c0:c0 + cw])
        hid = (gate * _sigmoid(gate) * up).astype(BF16)
        part = _dot(hid, wd_ref[c0:c0 + cw, :])
        y = part if y is None else y + part
    o_ref[...] = _layer_norm(ALPHA * x + y, g_ref[...], b_ref[...])


def _ffn(x2d, wg, wu, wd, g, b):
    t = x2d.shape[0]
    tm = ROW_TILE
    row = pl.BlockSpec((tm, D_MODEL), lambda i: (i, 0))
    consts = (wg, wu, wd, g, b)
    return pl.pallas_call(
        _ffn_kernel,
        grid=(t // tm,),
        in_specs=[row] + [_const_spec(c.shape) for c in consts],
        out_specs=row,
        out_shape=jax.ShapeDtypeStruct((t, D_MODEL), F32),
        compiler_params=_params(("arbitrary",)),
        name="ffn",
    )(x2d, *consts)


def _pad_head(parts):
    w = jnp.concatenate(parts, axis=-1)
    pad = HEAD_PAD - w.shape[-1]
    return jnp.pad(w, [(0, 0)] * (w.ndim - 1) + [(0, pad)])


def _rope_pair(w):
    x1, x2 = w[..., :HALF_ROPE], w[..., HALF_ROPE:]
    z = jnp.zeros(w.shape[:-1] + (QK_NOPE_DIM,), w.dtype)
    return _pad_head([z, x1, x2]), _pad_head([z, -x2, x1])


def _layer_weights(l, mix_w_in, mix_b_in, mla_q_norm, mla_w_uq, mla_kv_norm, mla_w_ukv, conv_dw_w,
                   conv_dw_b, conv_norm_g, conv_norm_b, mix_w_o, ln1_g, ln1_b, xa_w_q, xa_w_kv, xa_w_o,
                   ln2_g, ln2_b, ffn_w_in, ffn_w_down, ln3_g, ln3_b):
    row = lambda v: v[l][None, :].astype(F32)
    o1 = Q_LORA_RANK
    o2 = o1 + KV_LORA_RANK
    o3 = o2 + QK_ROPE_DIM
    w_in, b_in = mix_w_in[l], mix_b_in[l][None, :]
    kra_w, krb_w = _rope_pair(w_in[:, o2:o3])
    kra_b, krb_b = _rope_pair(b_in[:, o2:o3])
    w_in_ext = jnp.concatenate([w_in[:, :o2], kra_w, krb_w, w_in[:, o3:]], axis=1).astype(BF16)
    b_in_ext = jnp.concatenate([b_in[:, :o2], kra_b, krb_b, b_in[:, o3:]], axis=1).astype(F32)

    w_uq = mla_w_uq[l].reshape(Q_LORA_RANK, MLA_HEADS, QK_NOPE_DIM + QK_ROPE_DIM)
    q_nope, q_rope = w_uq[..., :QK_NOPE_DIM], w_uq[..., QK_NOPE_DIM:]
    qr_a, qr_b = _rope_pair(q_rope)
    wqa = (_pad_head([q_nope]) + qr_a).reshape(Q_LORA_RANK, QK_PAD).astype(BF16)
    wqb = qr_b.reshape(Q_LORA_RANK, QK_PAD).astype(BF16)

    w_ukv = mla_w_ukv[l].reshape(KV_LORA_RANK, MLA_HEADS, QK_NOPE_DIM + V_HEAD_DIM)
    wk = _pad_head([w_ukv[..., :QK_NOPE_DIM]]).reshape(KV_LORA_RANK, QK_PAD).astype(BF16)
    wv = _pad_head([w_ukv[..., QK_NOPE_DIM:]]).reshape(KV_LORA_RANK, QK_PAD).astype(BF16)

    return dict(
        mix_in=(w_in_ext, b_in_ext, row(mla_q_norm), row(mla_kv_norm), wqa, wqb, wk, wv),
        conv=(conv_dw_w[l].astype(F32), row(conv_dw_b), row(conv_norm_g), row(conv_norm_b)),
        mix_out=(mix_w_o[l][:MLA_WIDTH].astype(BF16), mix_w_o[l][MLA_WIDTH:].astype(BF16),
                 row(ln1_g), row(ln1_b), xa_w_q[l].astype(BF16)),
        xa_kv=(xa_w_kv[l][:, :D_MODEL].astype(BF16), xa_w_kv[l][:, D_MODEL:].astype(BF16)),
        xattn=(xa_w_o[l].astype(BF16), row(ln2_g), row(ln2_b)),
        ffn=(ffn_w_in[l][:, :FFN_HIDDEN].astype(BF16), ffn_w_in[l][:, FFN_HIDDEN:].astype(BF16),
             ffn_w_down[l].astype(BF16), row(ln3_g), row(ln3_b)),
    )


def _rope_constants():
    inv_freq = ROPE_BASE ** (-np.arange(0, QK_ROPE_DIM, 2, dtype=np.float32) / QK_ROPE_DIM)
    freq = np.zeros((1, HEAD_PAD), np.float32)
    freq[0, ROPE_LO:ROPE_LO + HALF_ROPE] = inv_freq
    freq[0, ROPE_LO + HALF_ROPE:ROPE_LO + QK_ROPE_DIM] = inv_freq
    vone = np.zeros((1, MLA_HEADS, HEAD_PAD), np.float32)
    vone[:, :, SUM_LANE] = 1.0
    return jnp.asarray(freq), jnp.asarray(vone.reshape(1, QK_PAD))


def kernel(x, mem, positions, mix_w_in, mix_b_in, mla_q_norm, mla_w_uq, mla_kv_norm, mla_w_ukv, conv_dw_w,
           conv_dw_b, conv_norm_g, conv_norm_b, mix_w_o, ln1_g, ln1_b, xa_w_q, xa_w_kv, xa_w_o, ln2_g, ln2_b,
           ffn_w_in, ffn_w_down, ln3_g, ln3_b):
    batch, seq, d = x.shape
    mem_len = mem.shape[1]
    assert d == D_MODEL and seq % ROW_TILE == 0 and seq % Q_TILE == 0 and ffn_w_down.shape[1] == FFN_HIDDEN
    assert (batch * mem_len) % ROW_TILE == 0
    weights = (mix_w_in, mix_b_in, mla_q_norm, mla_w_uq, mla_kv_norm, mla_w_ukv, conv_dw_w, conv_dw_b,
               conv_norm_g, conv_norm_b, mix_w_o, ln1_g, ln1_b, xa_w_q, xa_w_kv, xa_w_o, ln2_g, ln2_b,
               ffn_w_in, ffn_w_down, ln3_g, ln3_b)
    freq, vone = _rope_constants()
    h = x.reshape(batch * seq, d)
    pos2d = positions.reshape(batch * seq, 1)
    mem2d = mem.reshape(batch * mem_len, d)
    for l in range(DEPTH):
        w = _layer_weights(l, *weights)
        q, k, v, yc = _mix_in(h, pos2d, freq, *w["mix_in"], vone, *w["conv"], seq)
        o = _mla_attn(q, k, v, batch, seq)
        x1, qx = _mix_out(o, yc, h, *w["mix_out"])
        kx, vx = _xa_kv(mem2d, *w["xa_kv"])
        x2 = _xattn(qx, kx, vx, x1, *w["xattn"], seq, mem_len)
        h = _ffn(x2, *w["ffn"])
    return h.reshape(batch, seq, d)
```

```python
import functools

import jax
import jax.numpy as jnp
import numpy as np
from jax import lax
from jax.experimental import pallas as pl
from jax.experimental.pallas import tpu as pltpu

F32 = jnp.float32
BF16 = jnp.bfloat16

D_MODEL = 1024
DEPTH = 2
CHUNK = 64
MLA_HEADS = 8
QK_NOPE_DIM = 64
QK_ROPE_DIM = 32
V_HEAD_DIM = 64
Q_LORA_RANK = 256
KV_LORA_RANK = 256
MLA_WIDTH = MLA_HEADS * V_HEAD_DIM
CONV_WIDTH = D_MODEL - MLA_WIDTH
CONV_TAPS = 31
XA_HEADS = 4
XA_HEAD_DIM = D_MODEL // XA_HEADS
FFN_HIDDEN = 2816
ALPHA = (2.0 * DEPTH) ** 0.25
ROPE_BASE = 10000.0
LN_EPS = 1e-5
RMS_EPS = 1e-6

LANES = 128
SUBLANES = 8
VMEM_LIMIT_BYTES = 56 * 1024 * 1024

HEAD_PAD = LANES
HALF_ROPE = QK_ROPE_DIM // 2
ROPE_LO = QK_NOPE_DIM
QK_PAD = MLA_HEADS * HEAD_PAD
SUM_LANE = V_HEAD_DIM

ROW_TILE = 512
Q_TILE = 256
KV_TILE = 256
CONV_HALO = 32
CONV_ROWS = 32
FFN_CHUNKS = ((0, 1024), (1024, 1024), (2048, 768))

IN_EXT = Q_LORA_RANK + KV_LORA_RANK + 2 * HEAD_PAD + 2 * CONV_WIDTH
MASK_VALUE = -1e30
LOG2_E = 1.4426950408889634


def _dot(a, b):
    return jnp.dot(a, b, preferred_element_type=F32)


def _dot_nt(a, b):
    return lax.dot_general(a, b, (((1,), (1,)), ((), ())), preferred_element_type=F32)


def _rms_norm(x, g):
    return x * lax.rsqrt(jnp.mean(x * x, axis=-1, keepdims=True) + RMS_EPS) * g


def _layer_norm(x, g, b):
    mu = jnp.mean(x, axis=-1, keepdims=True)
    xc = x - mu
    var = jnp.mean(xc * xc, axis=-1, keepdims=True)
    return xc * lax.rsqrt(var + LN_EPS) * g + b


def _sigmoid(x):
    return 1.0 / (1.0 + jnp.exp(-x))


def _const_spec(shape):
    return pl.BlockSpec(shape, lambda *_: (0,) * len(shape))


def _params(semantics):
    return pltpu.CompilerParams(dimension_semantics=semantics, vmem_limit_bytes=VMEM_LIMIT_BYTES)


def _mix_in_kernel(tiles_per_seq, x_ref, pos_ref, freq_ref, w_in_ref, b_in_ref, qg_ref, kvg_ref,
                   wqa_ref, wqb_ref, wk_ref, wv_ref, vone_ref, dww_ref, dwb_ref, cng_ref, cnb_ref,
                   q_ref, k_ref, v_ref, yc_ref, hbuf_ref):
    rows = x_ref.shape[0]

    @pl.when(pl.program_id(0) % tiles_per_seq == 0)
    def _():
        hbuf_ref[0, 0:CONV_HALO, :] = jnp.zeros((CONV_HALO, CONV_WIDTH), F32)

    proj = _dot(x_ref[...].astype(BF16), w_in_ref[...]) + b_in_ref[...]
    o_kv = Q_LORA_RANK
    o_kr = o_kv + KV_LORA_RANK
    o_u = o_kr + 2 * HEAD_PAD
    c_q = proj[:, 0:o_kv]
    c_kv = proj[:, o_kv:o_kr]
    kr_a = proj[:, o_kr:o_kr + HEAD_PAD]
    kr_b = proj[:, o_kr + HEAD_PAD:o_u]
    u_a = proj[:, o_u:o_u + CONV_WIDTH]
    u_g = proj[:, o_u + CONV_WIDTH:o_u + 2 * CONV_WIDTH]

    ang = pos_ref[...].astype(F32) * freq_ref[...]
    cos = jnp.cos(ang)
    sin = jnp.sin(ang)

    scale = (QK_NOPE_DIM + QK_ROPE_DIM) ** -0.5 * LOG2_E
    cqn = _rms_norm(c_q, qg_ref[...]).astype(BF16)
    q_a = _dot(cqn, wqa_ref[...])
    q_b = _dot(cqn, wqb_ref[...])
    ckn = _rms_norm(c_kv, kvg_ref[...]).astype(BF16)
    k_n = _dot(ckn, wk_ref[...])
    k_r = kr_a * cos + kr_b * sin
    for h in range(MLA_HEADS):
        sl = slice(h * HEAD_PAD, (h + 1) * HEAD_PAD)
        q_ref[:, sl] = ((q_a[:, sl] * cos + q_b[:, sl] * sin) * scale).astype(BF16)
        k_ref[:, sl] = (k_n[:, sl] + k_r).astype(BF16)
    v_ref[...] = (_dot(ckn, wv_ref[...]) + vone_ref[...]).astype(BF16)

    hbuf_ref[0, CONV_HALO:CONV_HALO + rows, :] = u_a * _sigmoid(u_g)
    n_shift = rows + CONV_HALO - SUBLANES
    for s in range(1, SUBLANES):
        hbuf_ref[s, 0:n_shift, :] = hbuf_ref[0, s:s + n_shift, :]
    base = CONV_HALO - (CONV_TAPS - 1)
    for r0 in range(0, rows, CONV_ROWS):
        acc = jnp.zeros((CONV_ROWS, CONV_WIDTH), F32) + dwb_ref[...]
        for t in range(CONV_TAPS):
            s, a = (base + t) % SUBLANES, (base + t) // SUBLANES * SUBLANES
            w_t = jnp.concatenate([dww_ref[t]] * (CONV_ROWS // SUBLANES), axis=0)
            acc = acc + hbuf_ref[s, r0 + a:r0 + a + CONV_ROWS, :] * w_t
        y = _layer_norm(acc, cng_ref[...], cnb_ref[...])
        yc_ref[r0:r0 + CONV_ROWS, :] = (y * _sigmoid(y)).astype(BF16)
    hbuf_ref[0, 0:CONV_HALO, :] = hbuf_ref[0, rows:rows + CONV_HALO, :]


def _mix_in(x2d, pos2d, freq, w_in, b_in, qg, kvg, wqa, wqb, wk, wv, vone, dww, dwb, cng, cnb, seq):
    t = x2d.shape[0]
    tm = ROW_TILE
    row = lambda w: pl.BlockSpec((tm, w), lambda i: (i, 0))
    consts = (freq, w_in, b_in, qg, kvg, wqa, wqb, wk, wv, vone, dww, dwb, cng, cnb)
    return pl.pallas_call(
        functools.partial(_mix_in_kernel, seq // tm),
        grid=(t // tm,),
        in_specs=[row(D_MODEL), row(1)] + [_const_spec(c.shape) for c in consts],
        out_specs=[row(QK_PAD), row(QK_PAD), row(QK_PAD), row(CONV_WIDTH)],
        out_shape=[jax.ShapeDtypeStruct((t, QK_PAD), BF16), jax.ShapeDtypeStruct((t, QK_PAD), BF16),
                   jax.ShapeDtypeStruct((t, QK_PAD), BF16), jax.ShapeDtypeStruct((t, CONV_WIDTH), BF16)],
        scratch_shapes=[pltpu.VMEM((SUBLANES, tm + CONV_HALO, CONV_WIDTH), F32)],
        compiler_params=_params(("arbitrary",)),
        name="mix_in",
    )(x2d, pos2d, *consts)


def _mla_attn_kernel(q_ref, k_ref, v_ref, o_ref, m_ref, acc_ref):
    qi = pl.program_id(1)
    tq = q_ref.shape[0]
    r_chunk = lax.broadcasted_iota(jnp.int32, (tq, KV_TILE), 0) // CHUNK
    c_chunk = lax.broadcasted_iota(jnp.int32, (tq, KV_TILE), 1) // CHUNK
    diag_mask = c_chunk <= r_chunk
    lane = lax.broadcasted_iota(jnp.int32, (tq, HEAD_PAD), 1)

    m_ref[...] = jnp.full(m_ref.shape, MASK_VALUE, F32)
    acc_ref[...] = jnp.zeros(acc_ref.shape, F32)

    def step(j, masked):
        off = pl.multiple_of(j * KV_TILE, KV_TILE)

        def scores(h):
            sl = slice(h * HEAD_PAD, (h + 1) * HEAD_PAD)
            return _dot_nt(q_ref[:, sl], k_ref[pl.ds(off, KV_TILE), sl])

        s_next = scores(0)
        for h in range(MLA_HEADS):
            sl = slice(h * HEAD_PAD, (h + 1) * HEAD_PAD)
            s = s_next
            if h + 1 < MLA_HEADS:
                s_next = scores(h + 1)
            if masked:
                s = jnp.where(diag_mask, s, MASK_VALUE)
            m = m_ref[h]
            m_new = jnp.maximum(m, jnp.max(s, axis=1, keepdims=True))
            p = jnp.exp2(s - jnp.concatenate([m_new] * (KV_TILE // HEAD_PAD), axis=1))
            acc_ref[h] = jnp.exp2(m - m_new) * acc_ref[h] + _dot(p.astype(BF16), v_ref[pl.ds(off, KV_TILE), sl])
            m_ref[h] = m_new

    def body(j, carry):
        step(j, masked=False)
        return carry

    lax.fori_loop(0, qi, body, 0)
    lax.fori_loop(qi, qi + 1, lambda j, c: (step(j, masked=True), c)[1], 0)

    for hp in range(MLA_HEADS // 2):
        even = acc_ref[2 * hp]
        odd = acc_ref[2 * hp + 1]
        even = even / even[:, SUM_LANE:SUM_LANE + 1]
        odd = odd / odd[:, SUM_LANE:SUM_LANE + 1]
        pair = jnp.where(lane < V_HEAD_DIM, even, pltpu.roll(odd, V_HEAD_DIM, 1))
        o_ref[:, hp * LANES:(hp + 1) * LANES] = pair.astype(BF16)


def _mla_attn(q, k, v, batch, seq):
    assert Q_TILE == KV_TILE and Q_TILE % CHUNK == 0
    return pl.pallas_call(
        _mla_attn_kernel,
        grid=(batch, seq // Q_TILE),
        in_specs=[pl.BlockSpec((Q_TILE, QK_PAD), lambda b, i: (b * (seq // Q_TILE) + i, 0)),
                  pl.BlockSpec((seq, QK_PAD), lambda b, i: (b, 0)),
                  pl.BlockSpec((seq, QK_PAD), lambda b, i: (b, 0))],
        out_specs=pl.BlockSpec((Q_TILE, MLA_WIDTH), lambda b, i: (b * (seq // Q_TILE) + i, 0)),
        out_shape=jax.ShapeDtypeStruct((batch * seq, MLA_WIDTH), BF16),
        scratch_shapes=[pltpu.VMEM((MLA_HEADS, Q_TILE, HEAD_PAD), F32)] * 2,
        compiler_params=_params(("arbitrary", "arbitrary")),
        name="mla_attn",
    )(q, k, v)


def _mix_out_kernel(o_ref, yc_ref, x_ref, woa_ref, woc_ref, g_ref, b_ref, wq_ref, x1_ref, qx_ref):
    y = _dot(o_ref[...], woa_ref[...]) + _dot(yc_ref[...], woc_ref[...])
    x1 = _layer_norm(ALPHA * x_ref[...] + y, g_ref[...], b_ref[...])
    x1_ref[...] = x1
    qx_ref[...] = (_dot(x1.astype(BF16), wq_ref[...]) * (XA_HEAD_DIM ** -0.5 * LOG2_E)).astype(BF16)


def _mix_out(o, yc, x2d, woa, woc, g, b, wq):
    t = x2d.shape[0]
    tm = ROW_TILE
    row = lambda w: pl.BlockSpec((tm, w), lambda i: (i, 0))
    consts = (woa, woc, g, b, wq)
    return pl.pallas_call(
        _mix_out_kernel,
        grid=(t // tm,),
        in_specs=[row(MLA_WIDTH), row(CONV_WIDTH), row(D_MODEL)] + [_const_spec(c.shape) for c in consts],
        out_specs=[row(D_MODEL), row(D_MODEL)],
        out_shape=[jax.ShapeDtypeStruct((t, D_MODEL), F32), jax.ShapeDtypeStruct((t, D_MODEL), BF16)],
        compiler_params=_params(("arbitrary",)),
        name="mix_out",
    )(o, yc, x2d, *consts)


def _xa_kv_kernel(mem_ref, wk_ref, wv_ref, k_ref, v_ref):
    m = mem_ref[...].astype(BF16)
    k_ref[...] = _dot(m, wk_ref[...]).astype(BF16)
    v_ref[...] = _dot(m, wv_ref[...]).astype(BF16)


def _xa_kv(mem2d, wk, wv):
    t = mem2d.shape[0]
    tm = ROW_TILE
    row = pl.BlockSpec((tm, D_MODEL), lambda i: (i, 0))
    return pl.pallas_call(
        _xa_kv_kernel,
        grid=(t // tm,),
        in_specs=[row, _const_spec(wk.shape), _const_spec(wv.shape)],
        out_specs=[row, row],
        out_shape=[jax.ShapeDtypeStruct((t, D_MODEL), BF16)] * 2,
        compiler_params=_params(("arbitrary",)),
        name="xa_kv",
    )(mem2d, wk, wv)


def _xattn_kernel(qx_ref, kx_ref, vx_ref, x1_ref, wo_ref, g_ref, b_ref, x2_ref):
    def scores(h):
        sl = slice(h * XA_HEAD_DIM, (h + 1) * XA_HEAD_DIM)
        return _dot_nt(qx_ref[:, sl], kx_ref[:, sl])

    y = None
    s_next = scores(0)
    for h in range(XA_HEADS):
        sl = slice(h * XA_HEAD_DIM, (h + 1) * XA_HEAD_DIM)
        s = s_next
        if h + 1 < XA_HEADS:
            s_next = scores(h + 1)
        p = jnp.exp2(s - jnp.max(s, axis=1, keepdims=True))
        l = jnp.sum(p, axis=1, keepdims=True)
        o = _dot(p.astype(BF16), vx_ref[:, sl]) / l
        part = _dot(o.astype(BF16), wo_ref[sl, :])
        y = part if y is None else y + part
    x2_ref[...] = _layer_norm(ALPHA * x1_ref[...] + y, g_ref[...], b_ref[...])


def _xattn(qx, kx, vx, x1, wo, g, b, seq, mem_len):
    t = x1.shape[0]
    tm = ROW_TILE
    row = pl.BlockSpec((tm, D_MODEL), lambda i: (i, 0))
    mem = pl.BlockSpec((mem_len, D_MODEL), lambda i: (i // (seq // tm), 0))
    consts = (wo, g, b)
    return pl.pallas_call(
        _xattn_kernel,
        grid=(t // tm,),
        in_specs=[row, mem, mem, row] + [_const_spec(c.shape) for c in consts],
        out_specs=row,
        out_shape=jax.ShapeDtypeStruct((t, D_MODEL), F32),
        compiler_params=_params(("arbitrary",)),
        name="xattn",
    )(qx, kx, vx, x1, *consts)


def _ffn_kernel(x_ref, wg_ref, wu_ref, wd_ref, g_ref, b_ref, o_ref):
    x = x_ref[...]
    xb = x.astype(BF16)
    y = None
    for c0, cw in FFN_CHUNKS:
        gate = _dot(xb, wg_ref[:, c0:c0 + cw])
        up = _dot(xb, wu_ref[:, c0:c0 + cw])
        hid = (gate * _sigmoid(gate) * up).astype(BF16)
        part = _dot(hid, wd_ref[c0:c0 + cw, :])
        y = part if y is None else y + part
    o_ref[...] = _layer_norm(ALPHA * x + y, g_ref[...], b_ref[...])


def _ffn(x2d, wg, wu, wd, g, b):
    t = x2d.shape[0]
    tm = ROW_TILE
    row = pl.BlockSpec((tm, D_MODEL), lambda i: (i, 0))
    consts = (wg, wu, wd, g, b)
    return pl.pallas_call(
        _ffn_kernel,
        grid=(t // tm,),
        in_specs=[row] + [_const_spec(c.shape) for c in consts],
        out_specs=row,
        out_shape=jax.ShapeDtypeStruct((t, D_MODEL), F32),
        compiler_params=_params(("arbitrary",)),
        name="ffn",
    )(x2d, *consts)


def _pad_head(parts):
    w = jnp.concatenate(parts, axis=-1)
    pad = HEAD_PAD - w.shape[-1]
    return jnp.pad(w, [(0, 0)] * (w.ndim - 1) + [(0, pad)])


def _rope_pair(w):
    x1, x2 = w[..., :HALF_ROPE], w[..., HALF_ROPE:]
    z = jnp.zeros(w.shape[:-1] + (QK_NOPE_DIM,), w.dtype)
    return _pad_head([z, x1, x2]), _pad_head([z, -x2, x1])


def _layer_weights(l, mix_w_in, mix_b_in, mla_q_norm, mla_w_uq, mla_kv_norm, mla_w_ukv, conv_dw_w,
                   conv_dw_b, conv_norm_g, conv_norm_b, mix_w_o, ln1_g, ln1_b, xa_w_q, xa_w_kv, xa_w_o,
                   ln2_g, ln2_b, ffn_w_in, ffn_w_down, ln3_g, ln3_b):
    row = lambda v: v[l][None, :].astype(F32)
    o1 = Q_LORA_RANK
    o2 = o1 + KV_LORA_RANK
    o3 = o2 + QK_ROPE_DIM
    w_in, b_in = mix_w_in[l], mix_b_in[l][None, :]
    kra_w, krb_w = _rope_pair(w_in[:, o2:o3])
    kra_b, krb_b = _rope_pair(b_in[:, o2:o3])
    w_in_ext = jnp.concatenate([w_in[:, :o2], kra_w, krb_w, w_in[:, o3:]], axis=1).astype(BF16)
    b_in_ext = jnp.concatenate([b_in[:, :o2], kra_b, krb_b, b_in[:, o3:]], axis=1).astype(F32)

    w_uq = mla_w_uq[l].reshape(Q_LORA_RANK, MLA_HEADS, QK_NOPE_DIM + QK_ROPE_DIM)
    q_nope, q_rope = w_uq[..., :QK_NOPE_DIM], w_uq[..., QK_NOPE_DIM:]
    qr_a, qr_b = _rope_pair(q_rope)
    wqa = (_pad_head([q_nope]) + qr_a).reshape(Q_LORA_RANK, QK_PAD).astype(BF16)
    wqb = qr_b.reshape(Q_LORA_RANK, QK_PAD).astype(BF16)

    w_ukv = mla_w_ukv[l].reshape(KV_LORA_RANK, MLA_HEADS, QK_NOPE_DIM + V_HEAD_DIM)
    wk = _pad_head([w_ukv[..., :QK_NOPE_DIM]]).reshape(KV_LORA_RANK, QK_PAD).astype(BF16)
    wv = _pad_head([w_ukv[..., QK_NOPE_DIM:]]).reshape(KV_LORA_RANK, QK_PAD).astype(BF16)

    return dict(
        mix_in=(w_in_ext, b_in_ext, row(mla_q_norm), row(mla_kv_norm), wqa, wqb, wk, wv),
        conv=(jnp.broadcast_to(conv_dw_w[l].astype(F32)[:, None, :], (CONV_TAPS, SUBLANES, CONV_WIDTH)),
              row(conv_dw_b), row(conv_norm_g), row(conv_norm_b)),
        mix_out=(mix_w_o[l][:MLA_WIDTH].astype(BF16), mix_w_o[l][MLA_WIDTH:].astype(BF16),
                 row(ln1_g), row(ln1_b), xa_w_q[l].astype(BF16)),
        xa_kv=(xa_w_kv[l][:, :D_MODEL].astype(BF16), xa_w_kv[l][:, D_MODEL:].astype(BF16)),
        xattn=(xa_w_o[l].astype(BF16), row(ln2_g), row(ln2_b)),
        ffn=(ffn_w_in[l][:, :FFN_HIDDEN].astype(BF16), ffn_w_in[l][:, FFN_HIDDEN:].astype(BF16),
             ffn_w_down[l].astype(BF16), row(ln3_g), row(ln3_b)),
    )


def _rope_constants():
    inv_freq = ROPE_BASE ** (-np.arange(0, QK_ROPE_DIM, 2, dtype=np.float32) / QK_ROPE_DIM)
    freq = np.zeros((1, HEAD_PAD), np.float32)
    freq[0, ROPE_LO:ROPE_LO + HALF_ROPE] = inv_freq
    freq[0, ROPE_LO + HALF_ROPE:ROPE_LO + QK_ROPE_DIM] = inv_freq
    vone = np.zeros((1, MLA_HEADS, HEAD_PAD), np.float32)
    vone[:, :, SUM_LANE] = 1.0
    return jnp.asarray(freq), jnp.asarray(vone.reshape(1, QK_PAD))


def kernel(x, mem, positions, mix_w_in, mix_b_in, mla_q_norm, mla_w_uq, mla_kv_norm, mla_w_ukv, conv_dw_w,
           conv_dw_b, conv_norm_g, conv_norm_b, mix_w_o, ln1_g, ln1_b, xa_w_q, xa_w_kv, xa_w_o, ln2_g, ln2_b,
           ffn_w_in, ffn_w_down, ln3_g, ln3_b):
    batch, seq, d = x.shape
    mem_len = mem.shape[1]
    assert d == D_MODEL and seq % ROW_TILE == 0 and seq % Q_TILE == 0 and ffn_w_down.shape[1] == FFN_HIDDEN
    assert (batch * mem_len) % ROW_TILE == 0
    weights = (mix_w_in, mix_b_in, mla_q_norm, mla_w_uq, mla_kv_norm, mla_w_ukv, conv_dw_w, conv_dw_b,
               conv_norm_g, conv_norm_b, mix_w_o, ln1_g, ln1_b, xa_w_q, xa_w_kv, xa_w_o, ln2_g, ln2_b,
               ffn_w_in, ffn_w_down, ln3_g, ln3_b)
    freq, vone = _rope_constants()
    h = x.reshape(batch * seq, d)
    pos2d = positions.reshape(batch * seq, 1)
    mem2d = mem.reshape(batch * mem_len, d)
    for l in range(DEPTH):
        w = _layer_weights(l, *weights)
        q, k, v, yc = _mix_in(h, pos2d, freq, *w["mix_in"], vone, *w["conv"], seq)
        o = _mla_attn(q, k, v, batch, seq)
        x1, qx = _mix_out(o, yc, h, *w["mix_out"])
        kx, vx = _xa_kv(mem2d, *w["xa_kv"])
        x2 = _xattn(qx, kx, vx, x1, *w["xattn"], seq, mem_len)
        h = _ffn(x2, *w["ffn"])
    return h.reshape(batch, seq, d)
```

```python
import functools

import jax
import jax.numpy as jnp
import numpy as np
from jax import lax
from jax.experimental import pallas as pl
from jax.experimental.pallas import tpu as pltpu

F32 = jnp.float32
BF16 = jnp.bfloat16

D_MODEL = 1024
DEPTH = 2
CHUNK = 64
MLA_HEADS = 8
QK_NOPE_DIM = 64
QK_ROPE_DIM = 32
V_HEAD_DIM = 64
Q_LORA_RANK = 256
KV_LORA_RANK = 256
MLA_WIDTH = MLA_HEADS * V_HEAD_DIM
CONV_WIDTH = D_MODEL - MLA_WIDTH
CONV_TAPS = 31
XA_HEADS = 4
XA_HEAD_DIM = D_MODEL // XA_HEADS
FFN_HIDDEN = 2816
ALPHA = (2.0 * DEPTH) ** 0.25
ROPE_BASE = 10000.0
LN_EPS = 1e-5
RMS_EPS = 1e-6

LANES = 128
SUBLANES = 8
VMEM_LIMIT_BYTES = 56 * 1024 * 1024

HEAD_PAD = LANES
HALF_ROPE = QK_ROPE_DIM // 2
ROPE_LO = QK_NOPE_DIM
QK_PAD = MLA_HEADS * HEAD_PAD

ROW_TILE = 512
Q_TILE = 256
KV_TILE = 256
CONV_HALO = 32
CONV_ROWS = 32
ROPE_TABLE_ROWS = 2048
FFN_CHUNKS = ((0, 1024), (1024, 1024), (2048, 768))

IN_EXT = Q_LORA_RANK + KV_LORA_RANK + 2 * HEAD_PAD + 2 * CONV_WIDTH
MASK_VALUE = -1e30
LOG2_E = 1.4426950408889634


def _dot(a, b):
    return jnp.dot(a, b, preferred_element_type=F32)


def _dot_nt(a, b):
    return lax.dot_general(a, b, (((1,), (1,)), ((), ())), preferred_element_type=F32)


def _rms_norm(x, g):
    return x * lax.rsqrt(jnp.mean(x * x, axis=-1, keepdims=True) + RMS_EPS) * g


def _layer_norm(x, g, b):
    mu = jnp.mean(x, axis=-1, keepdims=True)
    xc = x - mu
    var = jnp.mean(xc * xc, axis=-1, keepdims=True)
    return xc * lax.rsqrt(var + LN_EPS) * g + b


def _sigmoid(x):
    return 1.0 / (1.0 + jnp.exp(-x))


def _const_spec(shape):
    return pl.BlockSpec(shape, lambda *_: (0,) * len(shape))


def _params(semantics):
    return pltpu.CompilerParams(dimension_semantics=semantics, vmem_limit_bytes=VMEM_LIMIT_BYTES)


def _rope_table_kernel(pos_ref, freq_ref, cos_ref, sin_ref):
    ang = pos_ref[...].astype(F32) * freq_ref[...]
    cos_ref[...] = jnp.cos(ang)
    sin_ref[...] = jnp.sin(ang)


def _rope_tables(positions):
    t = positions.size
    per_row = LANES // QK_ROPE_DIM
    rows = t // per_row
    inv_freq = ROPE_BASE ** (-np.arange(0, QK_ROPE_DIM, 2, dtype=np.float32) / QK_ROPE_DIM)
    freq = jnp.asarray(np.tile(inv_freq, 2 * per_row)[None, :])
    pos = jnp.repeat(positions.reshape(rows, per_row), QK_ROPE_DIM, axis=1)
    tile = min(rows, ROPE_TABLE_ROWS)
    spec = pl.BlockSpec((tile, LANES), lambda i: (i, 0))
    cos, sin = pl.pallas_call(
        _rope_table_kernel,
        grid=(rows // tile,),
        in_specs=[spec, _const_spec(freq.shape)],
        out_specs=[spec, spec],
        out_shape=[jax.ShapeDtypeStruct((rows, LANES), F32)] * 2,
        compiler_params=_params(("arbitrary",)),
        name="rope_table",
    )(pos, freq)
    lanes = ((0, 0), (ROPE_LO, HEAD_PAD - ROPE_LO - QK_ROPE_DIM))
    cos = jnp.pad(cos.reshape(t, QK_ROPE_DIM), lanes, constant_values=1.0)
    sin = jnp.pad(sin.reshape(t, QK_ROPE_DIM), lanes, constant_values=0.0)
    return cos, sin


def _mix_in_kernel(tiles_per_seq, x_ref, cos_ref, sin_ref, w_in_ref, b_in_ref, qg_ref, kvg_ref,
                   wqa_ref, wqb_ref, wk_ref, wv_ref, vone_ref, dww_ref, dwb_ref, cng_ref, cnb_ref,
                   q_ref, k_ref, v_ref, yc_ref, hbuf_ref):
    rows = x_ref.shape[0]

    @pl.when(pl.program_id(0) % tiles_per_seq == 0)
    def _():
        hbuf_ref[0, 0:CONV_HALO, :] = jnp.zeros((CONV_HALO, CONV_WIDTH), F32)

    proj = _dot(x_ref[...].astype(BF16), w_in_ref[...]) + b_in_ref[...]
    o_kv = Q_LORA_RANK
    o_kr = o_kv + KV_LORA_RANK
    o_u = o_kr + 2 * HEAD_PAD
    c_q = proj[:, 0:o_kv]
    c_kv = proj[:, o_kv:o_kr]
    kr_a = proj[:, o_kr:o_kr + HEAD_PAD]
    kr_b = proj[:, o_kr + HEAD_PAD:o_u]
    u_a = proj[:, o_u:o_u + CONV_WIDTH]
    u_g = proj[:, o_u + CONV_WIDTH:o_u + 2 * CONV_WIDTH]

    cos = cos_ref[...]
    sin = sin_ref[...]

    scale = (QK_NOPE_DIM + QK_ROPE_DIM) ** -0.5 * LOG2_E
    cqn = _rms_norm(c_q, qg_ref[...]).astype(BF16)
    q_a = _dot(cqn, wqa_ref[...])
    q_b = _dot(cqn, wqb_ref[...])
    ckn = _rms_norm(c_kv, kvg_ref[...]).astype(BF16)
    k_n = _dot(ckn, wk_ref[...])
    k_r = kr_a * cos + kr_b * sin
    cos_q = cos * scale
    sin_q = sin * scale
    for h in range(MLA_HEADS):
        sl = slice(h * HEAD_PAD, (h + 1) * HEAD_PAD)
        q_ref[:, sl] = (q_a[:, sl] * cos_q + q_b[:, sl] * sin_q).astype(BF16)
        k_ref[:, sl] = (k_n[:, sl] + k_r).astype(BF16)
    v_ref[...] = (_dot(ckn, wv_ref[...]) + vone_ref[...]).astype(BF16)

    hbuf_ref[0, CONV_HALO:CONV_HALO + rows, :] = u_a * _sigmoid(u_g)
    n_shift = rows + CONV_HALO - SUBLANES
    for s in range(1, SUBLANES):
        hbuf_ref[s, 0:n_shift, :] = hbuf_ref[0, s:s + n_shift, :]
    base = CONV_HALO - (CONV_TAPS - 1)
    for r0 in range(0, rows, CONV_ROWS):
        acc = jnp.zeros((CONV_ROWS, CONV_WIDTH), F32) + dwb_ref[...]
        for t in range(CONV_TAPS):
            s, a = (base + t) % SUBLANES, (base + t) // SUBLANES * SUBLANES
            w_t = jnp.concatenate([dww_ref[t]] * (CONV_ROWS // SUBLANES), axis=0)
            acc = acc + hbuf_ref[s, r0 + a:r0 + a + CONV_ROWS, :] * w_t
        y = _layer_norm(acc, cng_ref[...], cnb_ref[...])
        yc_ref[r0:r0 + CONV_ROWS, :] = (y * _sigmoid(y)).astype(BF16)
    hbuf_ref[0, 0:CONV_HALO, :] = hbuf_ref[0, rows:rows + CONV_HALO, :]


def _mix_in(x2d, cos, sin, w_in, b_in, qg, kvg, wqa, wqb, wk, wv, vone, dww, dwb, cng, cnb, seq):
    t = x2d.shape[0]
    tm = ROW_TILE
    row = lambda w: pl.BlockSpec((tm, w), lambda i: (i, 0))
    consts = (w_in, b_in, qg, kvg, wqa, wqb, wk, wv, vone, dww, dwb, cng, cnb)
    return pl.pallas_call(
        functools.partial(_mix_in_kernel, seq // tm),
        grid=(t // tm,),
        in_specs=[row(D_MODEL), row(HEAD_PAD), row(HEAD_PAD)] + [_const_spec(c.shape) for c in consts],
        out_specs=[row(QK_PAD), row(QK_PAD), row(QK_PAD), row(CONV_WIDTH)],
        out_shape=[jax.ShapeDtypeStruct((t, QK_PAD), BF16), jax.ShapeDtypeStruct((t, QK_PAD), BF16),
                   jax.ShapeDtypeStruct((t, QK_PAD), BF16), jax.ShapeDtypeStruct((t, CONV_WIDTH), BF16)],
        scratch_shapes=[pltpu.VMEM((SUBLANES, tm + CONV_HALO, CONV_WIDTH), F32)],
        compiler_params=_params(("arbitrary",)),
        name="mix_in",
    )(x2d, cos, sin, *consts)


def _mla_attn_kernel(q_ref, k_ref, v_ref, o_ref, m_ref, acc_ref):
    qi = pl.program_id(1)
    tq = q_ref.shape[0]
    r_chunk = lax.broadcasted_iota(jnp.int32, (tq, KV_TILE), 0) // CHUNK
    c_chunk = lax.broadcasted_iota(jnp.int32, (tq, KV_TILE), 1) // CHUNK
    diag_mask = c_chunk <= r_chunk
    low_half = lax.broadcasted_iota(jnp.int32, (tq, HEAD_PAD), 1) < V_HEAD_DIM

    m_ref[...] = jnp.full(m_ref.shape, MASK_VALUE, F32)
    acc_ref[...] = jnp.zeros(acc_ref.shape, F32)

    def head_lanes(h):
        return slice(h * HEAD_PAD, (h + 1) * HEAD_PAD)

    def run(units):
        def scores(unit):
            h, off, _ = unit
            return _dot_nt(q_ref[:, head_lanes(h)], k_ref[pl.ds(off, KV_TILE), head_lanes(h)])

        s_next = scores(units[0])
        for i, (h, off, masked) in enumerate(units):
            s = s_next
            if i + 1 < len(units):
                s_next = scores(units[i + 1])
            if masked:
                s = jnp.where(diag_mask, s, MASK_VALUE)
            m = m_ref[h]
            m_new = jnp.maximum(m, jnp.max(s, axis=1, keepdims=True))
            p = jnp.exp2(s - jnp.concatenate([m_new] * (KV_TILE // HEAD_PAD), axis=1))
            pv = _dot(p.astype(BF16), v_ref[pl.ds(off, KV_TILE), head_lanes(h)])
            acc_ref[h] = jnp.exp2(m - m_new) * acc_ref[h] + pv
            m_ref[h] = m_new

    def tile_units(off, masked):
        return [(h, off, masked) for h in range(MLA_HEADS)]

    def pair_body(jp, carry):
        off = pl.multiple_of(jp * (2 * KV_TILE), 2 * KV_TILE)
        run(tile_units(off, False) + tile_units(off + KV_TILE, False))
        return carry

    lax.fori_loop(0, qi // 2, pair_body, 0)
    diag_off = pl.multiple_of(qi * KV_TILE, KV_TILE)

    @pl.when(qi % 2 == 1)
    def _():
        run(tile_units(diag_off - KV_TILE, False) + tile_units(diag_off, True))

    @pl.when(qi % 2 == 0)
    def _():
        run(tile_units(diag_off, True))

    for hp in range(MLA_HEADS // 2):
        even = acc_ref[2 * hp]
        odd = acc_ref[2 * hp + 1]
        num = jnp.where(low_half, even, odd)
        den = pltpu.roll(jnp.where(low_half, odd, even), V_HEAD_DIM, 1)
        o_ref[:, hp * LANES:(hp + 1) * LANES] = (num / den).astype(BF16)


def _mla_attn(q, k, v, batch, seq):
    assert Q_TILE == KV_TILE and Q_TILE % CHUNK == 0
    return pl.pallas_call(
        _mla_attn_kernel,
        grid=(batch, seq // Q_TILE),
        in_specs=[pl.BlockSpec((Q_TILE, QK_PAD), lambda b, i: (b * (seq // Q_TILE) + i, 0)),
                  pl.BlockSpec((seq, QK_PAD), lambda b, i: (b, 0)),
                  pl.BlockSpec((seq, QK_PAD), lambda b, i: (b, 0))],
        out_specs=pl.BlockSpec((Q_TILE, MLA_WIDTH), lambda b, i: (b * (seq // Q_TILE) + i, 0)),
        out_shape=jax.ShapeDtypeStruct((batch * seq, MLA_WIDTH), BF16),
        scratch_shapes=[pltpu.VMEM((MLA_HEADS, Q_TILE, HEAD_PAD), F32)] * 2,
        compiler_params=_params(("arbitrary", "arbitrary")),
        name="mla_attn",
    )(q, k, v)


def _xa_kv_kernel(mem_ref, wk_ref, wv_ref, k_ref, v_ref):
    m = mem_ref[...].astype(BF16)
    k_ref[...] = _dot(m, wk_ref[...]).astype(BF16)
    v_ref[...] = _dot(m, wv_ref[...]).astype(BF16)


def _xa_kv(mem2d, wk, wv):
    t = mem2d.shape[0]
    tm = ROW_TILE
    row = pl.BlockSpec((tm, D_MODEL), lambda i: (i, 0))
    return pl.pallas_call(
        _xa_kv_kernel,
        grid=(t // tm,),
        in_specs=[row, _const_spec(wk.shape), _const_spec(wv.shape)],
        out_specs=[row, row],
        out_shape=[jax.ShapeDtypeStruct((t, D_MODEL), BF16)] * 2,
        compiler_params=_params(("arbitrary",)),
        name="xa_kv",
    )(mem2d, wk, wv)


def _post_attn_kernel(o_ref, yc_ref, x_ref, kx_ref, vx_ref, woa_ref, woc_ref, g1_ref, b1_ref, wq_ref,
                      wo_ref, g2_ref, b2_ref, x2_ref, x1_ref, qx_ref, ox_ref):
    half = x_ref.shape[0] // 2
    halves = (slice(0, half), slice(half, 2 * half))

    ys = [_dot(o_ref[r, :], woa_ref[...]) + _dot(yc_ref[r, :], woc_ref[...]) for r in halves]
    for r, y in zip(halves, ys):
        x1 = _layer_norm(ALPHA * x_ref[r, :] + y, g1_ref[...], b1_ref[...])
        x1_ref[r, :] = x1
        qx_ref[r, :] = (_dot(x1.astype(BF16), wq_ref[...]) * (XA_HEAD_DIM ** -0.5 * LOG2_E)).astype(BF16)

    def head_lanes(h):
        return slice(h * XA_HEAD_DIM, (h + 1) * XA_HEAD_DIM)

    def scores(unit):
        r, h = unit
        return _dot_nt(qx_ref[r, head_lanes(h)], kx_ref[:, head_lanes(h)])

    units = [(r, h) for h in range(XA_HEADS) for r in halves]
    s_next = scores(units[0])
    for i, (r, h) in enumerate(units):
        s = s_next
        if i + 1 < len(units):
            s_next = scores(units[i + 1])
        p = jnp.exp2(s - jnp.max(s, axis=1, keepdims=True))
        l = jnp.sum(p, axis=1, keepdims=True)
        ox_ref[r, head_lanes(h)] = (_dot(p.astype(BF16), vx_ref[:, head_lanes(h)]) / l).astype(BF16)

    ys = [_dot(ox_ref[r, :], wo_ref[...]) for r in halves]
    for r, y in zip(halves, ys):
        x2_ref[r, :] = _layer_norm(ALPHA * x1_ref[r, :] + y, g2_ref[...], b2_ref[...])


def _post_attn(o, yc, x2d, kx, vx, woa, woc, g1, b1, wq, wo, g2, b2, seq, mem_len):
    t = x2d.shape[0]
    tm = ROW_TILE
    row = lambda w: pl.BlockSpec((tm, w), lambda i: (i, 0))
    mem = pl.BlockSpec((mem_len, D_MODEL), lambda i: (i // (seq // tm), 0))
    consts = (woa, woc, g1, b1, wq, wo, g2, b2)
    return pl.pallas_call(
        _post_attn_kernel,
        grid=(t // tm,),
        in_specs=[row(MLA_WIDTH), row(CONV_WIDTH), row(D_MODEL), mem, mem] + [_const_spec(c.shape) for c in consts],
        out_specs=row(D_MODEL),
        out_shape=jax.ShapeDtypeStruct((t, D_MODEL), F32),
        scratch_shapes=[pltpu.VMEM((tm, D_MODEL), F32), pltpu.VMEM((tm, D_MODEL), BF16),
                        pltpu.VMEM((tm, D_MODEL), BF16)],
        compiler_params=_params(("arbitrary",)),
        name="post_attn",
    )(o, yc, x2d, kx, vx, *consts)


def _ffn_kernel(x_ref, wg_ref, wu_ref, wd_ref, g_ref, b_ref, o_ref):
    x = x_ref[...]
    xb = x.astype(BF16)

    def gate_up(c):
        c0, cw = FFN_CHUNKS[c]
        return _dot(xb, wg_ref[:, c0:c0 + cw]), _dot(xb, wu_ref[:, c0:c0 + cw])

    y = None
    gu_next = gate_up(0)
    for c, (c0, cw) in enumerate(FFN_CHUNKS):
        gate, up = gu_next
        if c + 1 < len(FFN_CHUNKS):
            gu_next = gate_up(c + 1)
        hid = (gate * _sigmoid(gate) * up).astype(BF16)
        part = _dot(hid, wd_ref[c0:c0 + cw, :])
        y = part if y is None else y + part
    o_ref[...] = _layer_norm(ALPHA * x + y, g_ref[...], b_ref[...])


def _ffn(x2d, wg, wu, wd, g, b):
    t = x2d.shape[0]
    tm = ROW_TILE
    row = pl.BlockSpec((tm, D_MODEL), lambda i: (i, 0))
    consts = (wg, wu, wd, g, b)
    return pl.pallas_call(
        _ffn_kernel,
        grid=(t // tm,),
        in_specs=[row] + [_const_spec(c.shape) for c in consts],
        out_specs=row,
        out_shape=jax.ShapeDtypeStruct((t, D_MODEL), F32),
        compiler_params=_params(("arbitrary",)),
        name="ffn",
    )(x2d, *consts)


def _pad_head(parts):
    w = jnp.concatenate(parts, axis=-1)
    pad = HEAD_PAD - w.shape[-1]
    return jnp.pad(w, [(0, 0)] * (w.ndim - 1) + [(0, pad)])


def _rope_pair(w):
    x1, x2 = w[..., :HALF_ROPE], w[..., HALF_ROPE:]
    z = jnp.zeros(w.shape[:-1] + (QK_NOPE_DIM,), w.dtype)
    return _pad_head([z, x1, x2]), _pad_head([z, -x2, x1])


def _layer_weights(l, mix_w_in, mix_b_in, mla_q_norm, mla_w_uq, mla_kv_norm, mla_w_ukv, conv_dw_w,
                   conv_dw_b, conv_norm_g, conv_norm_b, mix_w_o, ln1_g, ln1_b, xa_w_q, xa_w_kv, xa_w_o,
                   ln2_g, ln2_b, ffn_w_in, ffn_w_down, ln3_g, ln3_b):
    row = lambda v: v[l][None, :].astype(F32)
    o1 = Q_LORA_RANK
    o2 = o1 + KV_LORA_RANK
    o3 = o2 + QK_ROPE_DIM
    w_in, b_in = mix_w_in[l], mix_b_in[l][None, :]
    kra_w, krb_w = _rope_pair(w_in[:, o2:o3])
    kra_b, krb_b = _rope_pair(b_in[:, o2:o3])
    w_in_ext = jnp.concatenate([w_in[:, :o2], kra_w, krb_w, w_in[:, o3:]], axis=1).astype(BF16)
    b_in_ext = jnp.concatenate([b_in[:, :o2], kra_b, krb_b, b_in[:, o3:]], axis=1).astype(F32)

    w_uq = mla_w_uq[l].reshape(Q_LORA_RANK, MLA_HEADS, QK_NOPE_DIM + QK_ROPE_DIM)
    q_nope, q_rope = w_uq[..., :QK_NOPE_DIM], w_uq[..., QK_NOPE_DIM:]
    qr_a, qr_b = _rope_pair(q_rope)
    wqa = (_pad_head([q_nope]) + qr_a).reshape(Q_LORA_RANK, QK_PAD).astype(BF16)
    wqb = qr_b.reshape(Q_LORA_RANK, QK_PAD).astype(BF16)

    w_ukv = mla_w_ukv[l].reshape(KV_LORA_RANK, MLA_HEADS, QK_NOPE_DIM + V_HEAD_DIM)
    wk = _pad_head([w_ukv[..., :QK_NOPE_DIM]]).reshape(KV_LORA_RANK, QK_PAD).astype(BF16)
    w_uv = w_ukv[..., QK_NOPE_DIM:]
    zv = jnp.zeros_like(w_uv)
    wv = jnp.where((jnp.arange(MLA_HEADS) % 2 == 0)[None, :, None],
                   jnp.concatenate([w_uv, zv], axis=-1), jnp.concatenate([zv, w_uv], axis=-1))
    wv = wv.reshape(KV_LORA_RANK, QK_PAD).astype(BF16)

    return dict(
        mix_in=(w_in_ext, b_in_ext, row(mla_q_norm), row(mla_kv_norm), wqa, wqb, wk, wv),
        conv=(jnp.broadcast_to(conv_dw_w[l].astype(F32)[:, None, :], (CONV_TAPS, SUBLANES, CONV_WIDTH)),
              row(conv_dw_b), row(conv_norm_g), row(conv_norm_b)),
        xa_kv=(xa_w_kv[l][:, :D_MODEL].astype(BF16), xa_w_kv[l][:, D_MODEL:].astype(BF16)),
        post_attn=(mix_w_o[l][:MLA_WIDTH].astype(BF16), mix_w_o[l][MLA_WIDTH:].astype(BF16), row(ln1_g), row(ln1_b),
                   xa_w_q[l].astype(BF16), xa_w_o[l].astype(BF16), row(ln2_g), row(ln2_b)),
        ffn=(ffn_w_in[l][:, :FFN_HIDDEN].astype(BF16), ffn_w_in[l][:, FFN_HIDDEN:].astype(BF16),
             ffn_w_down[l].astype(BF16), row(ln3_g), row(ln3_b)),
    )


def _v_ones():
    vone = np.zeros((1, MLA_HEADS, HEAD_PAD), np.float32)
    vone[:, 0::2, V_HEAD_DIM:] = 1.0
    vone[:, 1::2, :V_HEAD_DIM] = 1.0
    return jnp.asarray(vone.reshape(1, QK_PAD))


def kernel(x, mem, positions, mix_w_in, mix_b_in, mla_q_norm, mla_w_uq, mla_kv_norm, mla_w_ukv, conv_dw_w,
           conv_dw_b, conv_norm_g, conv_norm_b, mix_w_o, ln1_g, ln1_b, xa_w_q, xa_w_kv, xa_w_o, ln2_g, ln2_b,
           ffn_w_in, ffn_w_down, ln3_g, ln3_b):
    batch, seq, d = x.shape
    mem_len = mem.shape[1]
    assert d == D_MODEL and seq % ROW_TILE == 0 and seq % Q_TILE == 0 and ffn_w_down.shape[1] == FFN_HIDDEN
    assert (batch * mem_len) % ROW_TILE == 0
    weights = (mix_w_in, mix_b_in, mla_q_norm, mla_w_uq, mla_kv_norm, mla_w_ukv, conv_dw_w, conv_dw_b,
               conv_norm_g, conv_norm_b, mix_w_o, ln1_g, ln1_b, xa_w_q, xa_w_kv, xa_w_o, ln2_g, ln2_b,
               ffn_w_in, ffn_w_down, ln3_g, ln3_b)
    vone = _v_ones()
    cos, sin = _rope_tables(positions)
    h = x.reshape(batch * seq, d)
    mem2d = mem.reshape(batch * mem_len, d)
    for l in range(DEPTH):
        w = _layer_weights(l, *weights)
        q, k, v, yc = _mix_in(h, cos, sin, *w["mix_in"], vone, *w["conv"], seq)
        o = _mla_attn(q, k, v, batch, seq)
        kx, vx = _xa_kv(mem2d, *w["xa_kv"])
        x2 = _post_attn(o, yc, h, kx, vx, *w["post_attn"], seq, mem_len)
        h = _ffn(x2, *w["ffn"])
    return h.reshape(batch, seq, d)
```

```python
import functools
import itertools

import jax
import jax.numpy as jnp
import numpy as np
from jax import lax
from jax.experimental import pallas as pl
from jax.experimental.pallas import tpu as pltpu

F32 = jnp.float32
BF16 = jnp.bfloat16

D_MODEL = 1024
DEPTH = 2
CHUNK = 64
MLA_HEADS = 8
QK_NOPE_DIM = 64
QK_ROPE_DIM = 32
V_HEAD_DIM = 64
Q_LORA_RANK = 256
KV_LORA_RANK = 256
MLA_WIDTH = MLA_HEADS * V_HEAD_DIM
CONV_WIDTH = D_MODEL - MLA_WIDTH
CONV_TAPS = 31
XA_HEADS = 4
XA_HEAD_DIM = D_MODEL // XA_HEADS
FFN_HIDDEN = 2816
ALPHA = (2.0 * DEPTH) ** 0.25
ROPE_BASE = 10000.0
LN_EPS = 1e-5
RMS_EPS = 1e-6

LANES = 128
SUBLANES = 8
VMEM_LIMIT_BYTES = 56 * 1024 * 1024

HEAD_PAD = LANES
HALF_ROPE = QK_ROPE_DIM // 2
ROPE_LO = QK_NOPE_DIM
QK_PAD = MLA_HEADS * HEAD_PAD

ROW_TILE = 512
Q_TILE = 256
KV_TILE = 256
CONV_HALO = 32
CONV_ROWS = 32
ROPE_TABLE_ROWS = 2048
FFN_CHUNKS = ((0, 1024), (1024, 1024), (2048, 768))
MXU_COLS = 256
FFN_DOTS_PER_CONV_PIECE = 2

IN_EXT = Q_LORA_RANK + KV_LORA_RANK + 2 * HEAD_PAD + 2 * CONV_WIDTH
MASK_VALUE = -1e30
LOG2_E = 1.4426950408889634


def _dot(a, b):
    return jnp.dot(a, b, preferred_element_type=F32)


def _dot_nt(a, b):
    return lax.dot_general(a, b, (((1,), (1,)), ((), ())), preferred_element_type=F32)


def _rms_norm(x, g):
    return x * lax.rsqrt(jnp.mean(x * x, axis=-1, keepdims=True) + RMS_EPS) * g


def _layer_norm(x, g, b):
    mu = jnp.mean(x, axis=-1, keepdims=True)
    xc = x - mu
    var = jnp.mean(xc * xc, axis=-1, keepdims=True)
    return xc * lax.rsqrt(var + LN_EPS) * g + b


def _sigmoid(x):
    return 1.0 / (1.0 + jnp.exp(-x))


def _const_spec(shape):
    return pl.BlockSpec(shape, lambda *_: (0,) * len(shape))


def _params(semantics):
    return pltpu.CompilerParams(dimension_semantics=semantics, vmem_limit_bytes=VMEM_LIMIT_BYTES)


def _rope_table_kernel(pos_ref, freq_ref, cos_ref, sin_ref):
    ang = pos_ref[...].astype(F32) * freq_ref[...]
    cos_ref[...] = jnp.cos(ang)
    sin_ref[...] = jnp.sin(ang)


def _rope_tables(positions):
    t = positions.size
    per_row = LANES // QK_ROPE_DIM
    rows = t // per_row
    inv_freq = ROPE_BASE ** (-np.arange(0, QK_ROPE_DIM, 2, dtype=np.float32) / QK_ROPE_DIM)
    freq = jnp.asarray(np.tile(inv_freq, 2 * per_row)[None, :])
    pos = jnp.repeat(positions.reshape(rows, per_row), QK_ROPE_DIM, axis=1)
    tile = min(rows, ROPE_TABLE_ROWS)
    spec = pl.BlockSpec((tile, LANES), lambda i: (i, 0))
    cos, sin = pl.pallas_call(
        _rope_table_kernel,
        grid=(rows // tile,),
        in_specs=[spec, _const_spec(freq.shape)],
        out_specs=[spec, spec],
        out_shape=[jax.ShapeDtypeStruct((rows, LANES), F32)] * 2,
        compiler_params=_params(("arbitrary",)),
        name="rope_table",
    )(pos, freq)
    lanes = ((0, 0), (ROPE_LO, HEAD_PAD - ROPE_LO - QK_ROPE_DIM))
    cos = jnp.pad(cos.reshape(t, QK_ROPE_DIM), lanes, constant_values=1.0)
    sin = jnp.pad(sin.reshape(t, QK_ROPE_DIM), lanes, constant_values=0.0)
    return cos, sin


def _mix_in_kernel(x_ref, cos_ref, sin_ref, w_in_ref, b_in_ref, qg_ref, kvg_ref,
                   wqa_ref, wqb_ref, wk_ref, wv_ref, vone_ref, q_ref, k_ref, v_ref, hg_ref):
    proj = _dot(x_ref[...].astype(BF16), w_in_ref[...]) + b_in_ref[...]
    o_kv = Q_LORA_RANK
    o_kr = o_kv + KV_LORA_RANK
    o_u = o_kr + 2 * HEAD_PAD
    c_q = proj[:, 0:o_kv]
    c_kv = proj[:, o_kv:o_kr]
    kr_a = proj[:, o_kr:o_kr + HEAD_PAD]
    kr_b = proj[:, o_kr + HEAD_PAD:o_u]
    u_a = proj[:, o_u:o_u + CONV_WIDTH]
    u_g = proj[:, o_u + CONV_WIDTH:o_u + 2 * CONV_WIDTH]

    cos = cos_ref[...]
    sin = sin_ref[...]

    scale = (QK_NOPE_DIM + QK_ROPE_DIM) ** -0.5 * LOG2_E
    cqn = _rms_norm(c_q, qg_ref[...]).astype(BF16)
    q_a = _dot(cqn, wqa_ref[...])
    q_b = _dot(cqn, wqb_ref[...])
    ckn = _rms_norm(c_kv, kvg_ref[...]).astype(BF16)
    k_n = _dot(ckn, wk_ref[...])
    k_r = kr_a * cos + kr_b * sin
    cos_q = cos * scale
    sin_q = sin * scale
    for h in range(MLA_HEADS):
        sl = slice(h * HEAD_PAD, (h + 1) * HEAD_PAD)
        q_ref[:, sl] = (q_a[:, sl] * cos_q + q_b[:, sl] * sin_q).astype(BF16)
        k_ref[:, sl] = (k_n[:, sl] + k_r).astype(BF16)
    v_ref[...] = (_dot(ckn, wv_ref[...]) + vone_ref[...]).astype(BF16)
    hg_ref[...] = u_a * _sigmoid(u_g)


def _mix_in(x2d, cos, sin, w_in, b_in, qg, kvg, wqa, wqb, wk, wv, vone):
    t = x2d.shape[0]
    tm = ROW_TILE
    row = lambda w: pl.BlockSpec((tm, w), lambda i: (i, 0))
    consts = (w_in, b_in, qg, kvg, wqa, wqb, wk, wv, vone)
    return pl.pallas_call(
        _mix_in_kernel,
        grid=(t // tm,),
        in_specs=[row(D_MODEL), row(HEAD_PAD), row(HEAD_PAD)] + [_const_spec(c.shape) for c in consts],
        out_specs=[row(QK_PAD), row(QK_PAD), row(QK_PAD), row(CONV_WIDTH)],
        out_shape=[jax.ShapeDtypeStruct((t, QK_PAD), BF16), jax.ShapeDtypeStruct((t, QK_PAD), BF16),
                   jax.ShapeDtypeStruct((t, QK_PAD), BF16), jax.ShapeDtypeStruct((t, CONV_WIDTH), F32)],
        compiler_params=_params(("arbitrary",)),
        name="mix_in",
    )(x2d, cos, sin, *consts)


def _mla_attn_kernel(q_ref, k_ref, v_ref, o_ref, m_ref, acc_ref):
    qi = pl.program_id(1)
    tq = q_ref.shape[0]
    r_chunk = lax.broadcasted_iota(jnp.int32, (tq, KV_TILE), 0) // CHUNK
    c_chunk = lax.broadcasted_iota(jnp.int32, (tq, KV_TILE), 1) // CHUNK
    diag_mask = c_chunk <= r_chunk
    low_half = lax.broadcasted_iota(jnp.int32, (tq, HEAD_PAD), 1) < V_HEAD_DIM

    m_ref[...] = jnp.full(m_ref.shape, MASK_VALUE, F32)
    acc_ref[...] = jnp.zeros(acc_ref.shape, F32)

    def head_lanes(h):
        return slice(h * HEAD_PAD, (h + 1) * HEAD_PAD)

    def run(units):
        def scores(unit):
            h, off, _ = unit
            return _dot_nt(q_ref[:, head_lanes(h)], k_ref[pl.ds(off, KV_TILE), head_lanes(h)])

        s_next = scores(units[0])
        for i, (h, off, masked) in enumerate(units):
            s = s_next
            if i + 1 < len(units):
                s_next = scores(units[i + 1])
            if masked:
                s = jnp.where(diag_mask, s, MASK_VALUE)
            m = m_ref[h]
            m_new = jnp.maximum(m, jnp.max(s, axis=1, keepdims=True))
            p = jnp.exp2(s - jnp.concatenate([m_new] * (KV_TILE // HEAD_PAD), axis=1))
            pv = _dot(p.astype(BF16), v_ref[pl.ds(off, KV_TILE), head_lanes(h)])
            acc_ref[h] = jnp.exp2(m - m_new) * acc_ref[h] + pv
            m_ref[h] = m_new

    def tile_units(off, masked):
        return [(h, off, masked) for h in range(MLA_HEADS)]

    def pair_body(jp, carry):
        off = pl.multiple_of(jp * (2 * KV_TILE), 2 * KV_TILE)
        run(tile_units(off, False) + tile_units(off + KV_TILE, False))
        return carry

    lax.fori_loop(0, qi // 2, pair_body, 0)
    diag_off = pl.multiple_of(qi * KV_TILE, KV_TILE)

    @pl.when(qi % 2 == 1)
    def _():
        run(tile_units(diag_off - KV_TILE, False) + tile_units(diag_off, True))

    @pl.when(qi % 2 == 0)
    def _():
        run(tile_units(diag_off, True))

    for hp in range(MLA_HEADS // 2):
        even = acc_ref[2 * hp]
        odd = acc_ref[2 * hp + 1]
        num = jnp.where(low_half, even, odd)
        den = pltpu.roll(jnp.where(low_half, odd, even), V_HEAD_DIM, 1)
        o_ref[:, hp * LANES:(hp + 1) * LANES] = (num / den).astype(BF16)


def _mla_attn(q, k, v, batch, seq):
    assert Q_TILE == KV_TILE and Q_TILE % CHUNK == 0
    return pl.pallas_call(
        _mla_attn_kernel,
        grid=(batch, seq // Q_TILE),
        in_specs=[pl.BlockSpec((Q_TILE, QK_PAD), lambda b, i: (b * (seq // Q_TILE) + i, 0)),
                  pl.BlockSpec((seq, QK_PAD), lambda b, i: (b, 0)),
                  pl.BlockSpec((seq, QK_PAD), lambda b, i: (b, 0))],
        out_specs=pl.BlockSpec((Q_TILE, MLA_WIDTH), lambda b, i: (b * (seq // Q_TILE) + i, 0)),
        out_shape=jax.ShapeDtypeStruct((batch * seq, MLA_WIDTH), BF16),
        scratch_shapes=[pltpu.VMEM((MLA_HEADS, Q_TILE, HEAD_PAD), F32)] * 2,
        compiler_params=_params(("arbitrary", "arbitrary")),
        name="mla_attn",
    )(q, k, v)


def _xa_kv_kernel(mem_ref, wk_ref, wv_ref, k_ref, v_ref):
    m = mem_ref[...].astype(BF16)
    k_ref[...] = _dot(m, wk_ref[...]).astype(BF16)
    v_ref[...] = _dot(m, wv_ref[...]).astype(BF16)


def _xa_kv(mem2d, wk, wv):
    t = mem2d.shape[0]
    tm = ROW_TILE
    row = pl.BlockSpec((tm, D_MODEL), lambda i: (i, 0))
    return pl.pallas_call(
        _xa_kv_kernel,
        grid=(t // tm,),
        in_specs=[row, _const_spec(wk.shape), _const_spec(wv.shape)],
        out_specs=[row, row],
        out_shape=[jax.ShapeDtypeStruct((t, D_MODEL), BF16)] * 2,
        compiler_params=_params(("arbitrary",)),
        name="xa_kv",
    )(mem2d, wk, wv)


def _conv_pieces(slot, first_in_seq, hg_ref, halo_ref, dww_ref, dwb_ref, cng_ref, cnb_ref, yc_ref, hs_ref):
    rows = hg_ref.shape[0]
    half = rows // 2
    base = CONV_HALO - (CONV_TAPS - 1)
    n_shift = half + CONV_HALO - SUBLANES

    def fill(r_half):
        if r_half == 0:
            halo = jnp.where(first_in_seq, 0.0, halo_ref[...])
        else:
            halo = hg_ref[r_half - CONV_HALO:r_half, :]
        hs_ref[0, 0:CONV_HALO, :] = halo
        hs_ref[0, CONV_HALO:CONV_HALO + half, :] = hg_ref[r_half:r_half + half, :]
        for s in range(1, SUBLANES):
            hs_ref[s, 0:n_shift, :] = hs_ref[0, s:s + n_shift, :]

    def chunk(r_half, r0):
        acc = jnp.zeros((CONV_ROWS, CONV_WIDTH), F32) + dwb_ref[...]
        for t in range(CONV_TAPS):
            s, a = (base + t) % SUBLANES, (base + t) // SUBLANES * SUBLANES
            w_t = jnp.concatenate([dww_ref[t]] * (CONV_ROWS // SUBLANES), axis=0)
            acc = acc + hs_ref[s, r0 + a:r0 + a + CONV_ROWS, :] * w_t
        y = _layer_norm(acc, cng_ref[...], cnb_ref[...])
        y = y * _sigmoid(y)
        yc_ref[slot, r_half + r0:r_half + r0 + CONV_ROWS, :] = y.astype(BF16)
        return y[0:SUBLANES, 0:LANES]

    pieces = []
    for r_half in (0, half):
        pieces.append(functools.partial(fill, r_half))
        pieces += [functools.partial(chunk, r_half, r0) for r0 in range(0, half, CONV_ROWS)]
    return pieces


def _tail_kernel(tiles_per_seq, n_tiles, o_ref, x_ref, hg_ref, halo_ref, kx_ref, vx_ref,
                 woa_ref, woc_ref, g1_ref, b1_ref, wq_ref, wo_ref, g2_ref, b2_ref,
                 wg_ref, wu_ref, wd_ref, g3_ref, b3_ref, dww_ref, dwb_ref, cng_ref, cnb_ref,
                 out_ref, yc_ref, hs_ref, qx_ref, ox_ref):
    step = pl.program_id(0)
    conv_tile = jnp.minimum(step, n_tiles - 1)

    @pl.when(step == 0)
    def _():
        yc_ref[...] = jnp.zeros(yc_ref.shape, BF16)

    pieces = iter(_conv_pieces(step % 2, conv_tile % tiles_per_seq == 0, hg_ref, halo_ref, dww_ref, dwb_ref,
                               cng_ref, cnb_ref, yc_ref, hs_ref))

    never = step < 0
    anchor = [None]

    def emit_conv(n):
        for piece in (pieces if n is None else itertools.islice(pieces, n)):
            token = piece()
            if token is not None:
                anchor[0] = token if anchor[0] is None else jnp.where(never, anchor[0], token)

    def anchored(v):
        if anchor[0] is None:
            return v
        token = jnp.concatenate([anchor[0]] * (v.shape[1] // LANES), axis=1)
        anchor[0] = None
        return jnp.concatenate([jnp.where(never, token, v[0:SUBLANES]), v[SUBLANES:]], axis=0)

    yc_prev = yc_ref.at[(step + 1) % 2]
    half = x_ref.shape[0] // 2
    halves = (slice(0, half), slice(half, 2 * half))

    ys = [_dot(o_ref[r, :], woa_ref[...]) + _dot(yc_prev[r, :], woc_ref[...]) for r in halves]
    emit_conv(2)
    for r, y in zip(halves, ys):
        x1 = _layer_norm(ALPHA * x_ref[r, :] + y, g1_ref[...], b1_ref[...])
        out_ref[r, :] = x1
        qx_ref[r, :] = (_dot(x1.astype(BF16), wq_ref[...]) * (XA_HEAD_DIM ** -0.5 * LOG2_E)).astype(BF16)
    emit_conv(1)

    def head_lanes(h):
        return slice(h * XA_HEAD_DIM, (h + 1) * XA_HEAD_DIM)

    def scores(unit):
        r, h = unit
        return _dot_nt(qx_ref[r, head_lanes(h)], kx_ref[:, head_lanes(h)])

    units = [(r, h) for h in range(XA_HEADS) for r in halves]
    s_next = scores(units[0])
    for i, (r, h) in enumerate(units):
        s = s_next
        if i + 1 < len(units):
            s_next = scores(units[i + 1])
        p = jnp.exp2(s - jnp.max(s, axis=1, keepdims=True))
        l = jnp.sum(p, axis=1, keepdims=True)
        ox_ref[r, head_lanes(h)] = (_dot(p.astype(BF16), vx_ref[:, head_lanes(h)]) / l).astype(BF16)
    emit_conv(1)

    ys = [_dot(ox_ref[r, :], wo_ref[...]) for r in halves]
    emit_conv(2)
    for r, y in zip(halves, ys):
        out_ref[r, :] = _layer_norm(ALPHA * out_ref[r, :] + y, g2_ref[...], b2_ref[...])

    x2 = out_ref[...]
    xb = x2.astype(BF16)
    hid_refs = (qx_ref, ox_ref)
    dots_issued = [0]

    def after_dot():
        dots_issued[0] += 1
        if dots_issued[0] % FFN_DOTS_PER_CONV_PIECE == 0:
            emit_conv(1)

    def gate_up(c):
        c0, cw = FFN_CHUNKS[c]
        for j in range(0, cw, MXU_COLS):
            cols = slice(c0 + j, c0 + j + MXU_COLS)
            gate = _dot(xb, wg_ref[:, cols])
            after_dot()
            up = _dot(xb, wu_ref[:, cols])
            after_dot()
            hid_refs[c % 2][:, j:j + MXU_COLS] = anchored(gate * _sigmoid(gate) * up).astype(BF16)

    ys = [None] * (D_MODEL // MXU_COLS)

    def down(c):
        c0, cw = FFN_CHUNKS[c]
        hid = hid_refs[c % 2][:, 0:cw]
        for n in range(len(ys)):
            part = _dot(hid, wd_ref[c0:c0 + cw, n * MXU_COLS:(n + 1) * MXU_COLS])
            after_dot()
            ys[n] = part if ys[n] is None else ys[n] + part

    gate_up(0)
    for c in range(len(FFN_CHUNKS)):
        if c + 1 < len(FFN_CHUNKS):
            gate_up(c + 1)
        down(c)
    emit_conv(None)
    out_ref[...] = _layer_norm(ALPHA * x2 + jnp.concatenate(ys, axis=1), g3_ref[...], b3_ref[...])


def _tail(o, x2d, hg, kx, vx, weights, seq, mem_len):
    t = x2d.shape[0]
    tm = ROW_TILE
    n_tiles = t // tm
    halo_per_tile = tm // CONV_HALO
    prev = lambda s: jnp.maximum(s - 1, 0)
    cur = lambda s: jnp.minimum(s, n_tiles - 1)
    row_prev = lambda w: pl.BlockSpec((tm, w), lambda s: (prev(s), 0))
    in_specs = [row_prev(MLA_WIDTH), row_prev(D_MODEL),
                pl.BlockSpec((tm, CONV_WIDTH), lambda s: (cur(s), 0)),
                pl.BlockSpec((CONV_HALO, CONV_WIDTH), lambda s: (jnp.maximum(cur(s) * halo_per_tile - 1, 0), 0)),
                pl.BlockSpec((mem_len, D_MODEL), lambda s: (prev(s) // (seq // tm), 0)),
                pl.BlockSpec((mem_len, D_MODEL), lambda s: (prev(s) // (seq // tm), 0))]
    return pl.pallas_call(
        functools.partial(_tail_kernel, seq // tm, n_tiles),
        grid=(n_tiles + 1,),
        in_specs=in_specs + [_const_spec(c.shape) for c in weights],
        out_specs=row_prev(D_MODEL),
        out_shape=jax.ShapeDtypeStruct((t, D_MODEL), F32),
        scratch_shapes=[pltpu.VMEM((2, tm, CONV_WIDTH), BF16),
                        pltpu.VMEM((SUBLANES, tm // 2 + CONV_HALO, CONV_WIDTH), F32),
                        pltpu.VMEM((tm, D_MODEL), BF16), pltpu.VMEM((tm, D_MODEL), BF16)],
        compiler_params=_params(("arbitrary",)),
        name="tail",
    )(o, x2d, hg, hg, kx, vx, *weights)


def _pad_head(parts):
    w = jnp.concatenate(parts, axis=-1)
    pad = HEAD_PAD - w.shape[-1]
    return jnp.pad(w, [(0, 0)] * (w.ndim - 1) + [(0, pad)])


def _rope_pair(w):
    x1, x2 = w[..., :HALF_ROPE], w[..., HALF_ROPE:]
    z = jnp.zeros(w.shape[:-1] + (QK_NOPE_DIM,), w.dtype)
    return _pad_head([z, x1, x2]), _pad_head([z, -x2, x1])


def _layer_weights(l, mix_w_in, mix_b_in, mla_q_norm, mla_w_uq, mla_kv_norm, mla_w_ukv, conv_dw_w,
                   conv_dw_b, conv_norm_g, conv_norm_b, mix_w_o, ln1_g, ln1_b, xa_w_q, xa_w_kv, xa_w_o,
                   ln2_g, ln2_b, ffn_w_in, ffn_w_down, ln3_g, ln3_b):
    row = lambda v: v[l][None, :].astype(F32)
    o1 = Q_LORA_RANK
    o2 = o1 + KV_LORA_RANK
    o3 = o2 + QK_ROPE_DIM
    w_in, b_in = mix_w_in[l], mix_b_in[l][None, :]
    kra_w, krb_w = _rope_pair(w_in[:, o2:o3])
    kra_b, krb_b = _rope_pair(b_in[:, o2:o3])
    w_in_ext = jnp.concatenate([w_in[:, :o2], kra_w, krb_w, w_in[:, o3:]], axis=1).astype(BF16)
    b_in_ext = jnp.concatenate([b_in[:, :o2], kra_b, krb_b, b_in[:, o3:]], axis=1).astype(F32)

    w_uq = mla_w_uq[l].reshape(Q_LORA_RANK, MLA_HEADS, QK_NOPE_DIM + QK_ROPE_DIM)
    q_nope, q_rope = w_uq[..., :QK_NOPE_DIM], w_uq[..., QK_NOPE_DIM:]
    qr_a, qr_b = _rope_pair(q_rope)
    wqa = (_pad_head([q_nope]) + qr_a).reshape(Q_LORA_RANK, QK_PAD).astype(BF16)
    wqb = qr_b.reshape(Q_LORA_RANK, QK_PAD).astype(BF16)

    w_ukv = mla_w_ukv[l].reshape(KV_LORA_RANK, MLA_HEADS, QK_NOPE_DIM + V_HEAD_DIM)
    wk = _pad_head([w_ukv[..., :QK_NOPE_DIM]]).reshape(KV_LORA_RANK, QK_PAD).astype(BF16)
    w_uv = w_ukv[..., QK_NOPE_DIM:]
    zv = jnp.zeros_like(w_uv)
    wv = jnp.where((jnp.arange(MLA_HEADS) % 2 == 0)[None, :, None],
                   jnp.concatenate([w_uv, zv], axis=-1), jnp.concatenate([zv, w_uv], axis=-1))
    wv = wv.reshape(KV_LORA_RANK, QK_PAD).astype(BF16)

    return dict(
        mix_in=(w_in_ext, b_in_ext, row(mla_q_norm), row(mla_kv_norm), wqa, wqb, wk, wv),
        xa_kv=(xa_w_kv[l][:, :D_MODEL].astype(BF16), xa_w_kv[l][:, D_MODEL:].astype(BF16)),
        tail=(mix_w_o[l][:MLA_WIDTH].astype(BF16), mix_w_o[l][MLA_WIDTH:].astype(BF16), row(ln1_g), row(ln1_b),
              xa_w_q[l].astype(BF16), xa_w_o[l].astype(BF16), row(ln2_g), row(ln2_b),
              ffn_w_in[l][:, :FFN_HIDDEN].astype(BF16), ffn_w_in[l][:, FFN_HIDDEN:].astype(BF16),
              ffn_w_down[l].astype(BF16), row(ln3_g), row(ln3_b),
              jnp.broadcast_to(conv_dw_w[l].astype(F32)[:, None, :], (CONV_TAPS, SUBLANES, CONV_WIDTH)),
              row(conv_dw_b), row(conv_norm_g), row(conv_norm_b)),
    )


def _v_ones():
    vone = np.zeros((1, MLA_HEADS, HEAD_PAD), np.float32)
    vone[:, 0::2, V_HEAD_DIM:] = 1.0
    vone[:, 1::2, :V_HEAD_DIM] = 1.0
    return jnp.asarray(vone.reshape(1, QK_PAD))


def kernel(x, mem, positions, mix_w_in, mix_b_in, mla_q_norm, mla_w_uq, mla_kv_norm, mla_w_ukv, conv_dw_w,
           conv_dw_b, conv_norm_g, conv_norm_b, mix_w_o, ln1_g, ln1_b, xa_w_q, xa_w_kv, xa_w_o, ln2_g, ln2_b,
           ffn_w_in, ffn_w_down, ln3_g, ln3_b):
    batch, seq, d = x.shape
    mem_len = mem.shape[1]
    assert d == D_MODEL and seq % ROW_TILE == 0 and seq % Q_TILE == 0 and ffn_w_down.shape[1] == FFN_HIDDEN
    assert (batch * mem_len) % ROW_TILE == 0 and ROW_TILE % (2 * CONV_ROWS) == 0 and ROW_TILE // 2 >= CONV_HALO
    weights = (mix_w_in, mix_b_in, mla_q_norm, mla_w_uq, mla_kv_norm, mla_w_ukv, conv_dw_w, conv_dw_b,
               conv_norm_g, conv_norm_b, mix_w_o, ln1_g, ln1_b, xa_w_q, xa_w_kv, xa_w_o, ln2_g, ln2_b,
               ffn_w_in, ffn_w_down, ln3_g, ln3_b)
    vone = _v_ones()
    cos, sin = _rope_tables(positions)
    h = x.reshape(batch * seq, d)
    mem2d = mem.reshape(batch * mem_len, d)
    for l in range(DEPTH):
        w = _layer_weights(l, *weights)
        q, k, v, hg = _mix_in(h, cos, sin, *w["mix_in"], vone)
        o = _mla_attn(q, k, v, batch, seq)
        kx, vx = _xa_kv(mem2d, *w["xa_kv"])
        h = _tail(o, h, hg, kx, vx, w["tail"], seq, mem_len)
    return h.reshape(batch, seq, d)
```

```python
import functools
import itertools

import jax
import jax.numpy as jnp
import numpy as np
from jax import lax
from jax.experimental import pallas as pl
from jax.experimental.pallas import tpu as pltpu

F32 = jnp.float32
BF16 = jnp.bfloat16

D_MODEL = 1024
DEPTH = 2
CHUNK = 64
MLA_HEADS = 8
QK_NOPE_DIM = 64
QK_ROPE_DIM = 32
V_HEAD_DIM = 64
Q_LORA_RANK = 256
KV_LORA_RANK = 256
MLA_WIDTH = MLA_HEADS * V_HEAD_DIM
CONV_WIDTH = D_MODEL - MLA_WIDTH
CONV_TAPS = 31
XA_HEADS = 4
XA_HEAD_DIM = D_MODEL // XA_HEADS
FFN_HIDDEN = 2816
ALPHA = (2.0 * DEPTH) ** 0.25
ROPE_BASE = 10000.0
LN_EPS = 1e-5
RMS_EPS = 1e-6

LANES = 128
SUBLANES = 8
VMEM_LIMIT_BYTES = 56 * 1024 * 1024

HEAD_PAD = LANES
HALF_ROPE = QK_ROPE_DIM // 2
ROPE_LO = QK_NOPE_DIM
QK_PAD = MLA_HEADS * HEAD_PAD

ROW_TILE = 512
Q_TILE = 256
KV_TILE = 256
ATTN_TILES_PER_STEP = 4
CONV_HALO = 32
CONV_ROWS = 32
ROPE_TABLE_ROWS = 2048
FFN_CHUNKS = ((0, 1024), (1024, 1024), (2048, 768))
MXU_COLS = 256
FFN_DOTS_PER_CONV_PIECE = 1
CONV_PIECES_BEFORE_SWIGLU = (0, 0, 0, 0)

IN_EXT = Q_LORA_RANK + KV_LORA_RANK + 2 * HEAD_PAD + 2 * CONV_WIDTH
MASK_VALUE = -1e30
LOG2_E = 1.4426950408889634


def _dot(a, b):
    return jnp.dot(a, b, preferred_element_type=F32)


def _dot_nt(a, b):
    return lax.dot_general(a, b, (((1,), (1,)), ((), ())), preferred_element_type=F32)


def _rms_norm(x, g):
    return x * lax.rsqrt(jnp.mean(x * x, axis=-1, keepdims=True) + RMS_EPS) * g


def _layer_norm(x, g, b):
    mu = jnp.mean(x, axis=-1, keepdims=True)
    xc = x - mu
    var = jnp.mean(xc * xc, axis=-1, keepdims=True)
    return xc * lax.rsqrt(var + LN_EPS) * g + b


def _sigmoid(x):
    return 1.0 / (1.0 + jnp.exp(-x))


def _const_spec(shape):
    return pl.BlockSpec(shape, lambda *_: (0,) * len(shape))


def _params(semantics):
    return pltpu.CompilerParams(dimension_semantics=semantics, vmem_limit_bytes=VMEM_LIMIT_BYTES)


def _rope_table_kernel(pos_ref, freq_ref, cos_ref, sin_ref):
    ang = pos_ref[...].astype(F32) * freq_ref[...]
    cos_ref[...] = jnp.cos(ang)
    sin_ref[...] = jnp.sin(ang)


def _rope_tables(positions):
    t = positions.size
    per_row = LANES // QK_ROPE_DIM
    rows = t // per_row
    inv_freq = ROPE_BASE ** (-np.arange(0, QK_ROPE_DIM, 2, dtype=np.float32) / QK_ROPE_DIM)
    freq = jnp.asarray(np.tile(inv_freq, 2 * per_row)[None, :])
    pos = jnp.repeat(positions.reshape(rows, per_row), QK_ROPE_DIM, axis=1)
    tile = min(rows, ROPE_TABLE_ROWS)
    spec = pl.BlockSpec((tile, LANES), lambda i: (i, 0))
    cos, sin = pl.pallas_call(
        _rope_table_kernel,
        grid=(rows // tile,),
        in_specs=[spec, _const_spec(freq.shape)],
        out_specs=[spec, spec],
        out_shape=[jax.ShapeDtypeStruct((rows, LANES), F32)] * 2,
        compiler_params=_params(("arbitrary",)),
        name="rope_table",
    )(pos, freq)
    lanes = ((0, 0), (ROPE_LO, HEAD_PAD - ROPE_LO - QK_ROPE_DIM))
    cos = jnp.pad(cos.reshape(t, QK_ROPE_DIM), lanes, constant_values=1.0)
    sin = jnp.pad(sin.reshape(t, QK_ROPE_DIM), lanes, constant_values=0.0)
    return cos, sin


def _mix_in_kernel(x_ref, cos_ref, sin_ref, w_in_ref, b_in_ref, qg_ref, kvg_ref,
                   wqa_ref, wqb_ref, wk_ref, wv_ref, vone_ref, q_ref, k_ref, v_ref, hg_ref):
    proj = _dot(x_ref[...].astype(BF16), w_in_ref[...]) + b_in_ref[...]
    o_kv = Q_LORA_RANK
    o_kr = o_kv + KV_LORA_RANK
    o_u = o_kr + 2 * HEAD_PAD
    c_q = proj[:, 0:o_kv]
    c_kv = proj[:, o_kv:o_kr]
    kr_a = proj[:, o_kr:o_kr + HEAD_PAD]
    kr_b = proj[:, o_kr + HEAD_PAD:o_u]
    u_a = proj[:, o_u:o_u + CONV_WIDTH]
    u_g = proj[:, o_u + CONV_WIDTH:o_u + 2 * CONV_WIDTH]

    cos = cos_ref[...]
    sin = sin_ref[...]

    scale = (QK_NOPE_DIM + QK_ROPE_DIM) ** -0.5 * LOG2_E
    cqn = _rms_norm(c_q, qg_ref[...]).astype(BF16)
    q_a = _dot(cqn, wqa_ref[...])
    q_b = _dot(cqn, wqb_ref[...])
    ckn = _rms_norm(c_kv, kvg_ref[...]).astype(BF16)
    k_n = _dot(ckn, wk_ref[...])
    k_r = kr_a * cos + kr_b * sin
    cos_q = cos * scale
    sin_q = sin * scale
    for h in range(MLA_HEADS):
        sl = slice(h * HEAD_PAD, (h + 1) * HEAD_PAD)
        q_ref[:, sl] = (q_a[:, sl] * cos_q + q_b[:, sl] * sin_q).astype(BF16)
        k_ref[:, sl] = (k_n[:, sl] + k_r).astype(BF16)
    v_ref[...] = (_dot(ckn, wv_ref[...]) + vone_ref[...]).astype(BF16)
    hg_ref[...] = u_a * _sigmoid(u_g)


def _mix_in(x2d, cos, sin, w_in, b_in, qg, kvg, wqa, wqb, wk, wv, vone):
    t = x2d.shape[0]
    tm = ROW_TILE
    row = lambda w: pl.BlockSpec((tm, w), lambda i: (i, 0))
    consts = (w_in, b_in, qg, kvg, wqa, wqb, wk, wv, vone)
    return pl.pallas_call(
        _mix_in_kernel,
        grid=(t // tm,),
        in_specs=[row(D_MODEL), row(HEAD_PAD), row(HEAD_PAD)] + [_const_spec(c.shape) for c in consts],
        out_specs=[row(QK_PAD), row(QK_PAD), row(QK_PAD), row(CONV_WIDTH)],
        out_shape=[jax.ShapeDtypeStruct((t, QK_PAD), BF16), jax.ShapeDtypeStruct((t, QK_PAD), BF16),
                   jax.ShapeDtypeStruct((t, QK_PAD), BF16), jax.ShapeDtypeStruct((t, CONV_WIDTH), F32)],
        compiler_params=_params(("arbitrary",)),
        name="mix_in",
    )(x2d, cos, sin, *consts)


def _mla_attn_kernel(q_ref, k_ref, v_ref, o_ref, m_ref, acc_ref):
    qi = pl.program_id(1)
    tq = q_ref.shape[0]
    r_chunk = lax.broadcasted_iota(jnp.int32, (tq, KV_TILE), 0) // CHUNK
    c_chunk = lax.broadcasted_iota(jnp.int32, (tq, KV_TILE), 1) // CHUNK
    diag_mask = c_chunk <= r_chunk
    low_half = lax.broadcasted_iota(jnp.int32, (tq, HEAD_PAD), 1) < V_HEAD_DIM

    m_ref[...] = jnp.full(m_ref.shape, MASK_VALUE, F32)
    acc_ref[...] = jnp.zeros(acc_ref.shape, F32)

    def head_lanes(h):
        return slice(h * HEAD_PAD, (h + 1) * HEAD_PAD)

    def run(units):
        def scores(unit):
            h, off, _ = unit
            return _dot_nt(q_ref[:, head_lanes(h)], k_ref[pl.ds(off, KV_TILE), head_lanes(h)])

        s_next = scores(units[0])
        for i, (h, off, masked) in enumerate(units):
            s = s_next
            if i + 1 < len(units):
                s_next = scores(units[i + 1])
            if masked:
                s = jnp.where(diag_mask, s, MASK_VALUE)
            m = m_ref[h]
            m_new = jnp.maximum(m, jnp.max(s, axis=1, keepdims=True))
            p = jnp.exp2(s - jnp.concatenate([m_new] * (KV_TILE // HEAD_PAD), axis=1))
            pv = _dot(p.astype(BF16), v_ref[pl.ds(off, KV_TILE), head_lanes(h)])
            acc_ref[h] = jnp.exp2(m - m_new) * acc_ref[h] + pv
            m_ref[h] = m_new

    def tile_units(off, masked):
        return [(h, off, masked) for h in range(MLA_HEADS)]

    def full_tiles(off, n):
        return [unit for t in range(n) for unit in tile_units(off + t * KV_TILE, False)]

    def group_body(jg, carry):
        run(full_tiles(pl.multiple_of(jg * (ATTN_TILES_PER_STEP * KV_TILE), KV_TILE), ATTN_TILES_PER_STEP))
        return carry

    lax.fori_loop(0, qi // ATTN_TILES_PER_STEP, group_body, 0)
    left_over = qi % ATTN_TILES_PER_STEP
    for n_left in range(ATTN_TILES_PER_STEP):
        @pl.when(left_over == n_left)
        def _(n_left=n_left):
            off = pl.multiple_of((qi - n_left) * KV_TILE, KV_TILE)
            run(full_tiles(off, n_left) + tile_units(off + n_left * KV_TILE, True))

    for hp in range(MLA_HEADS // 2):
        even = acc_ref[2 * hp]
        odd = acc_ref[2 * hp + 1]
        num = jnp.where(low_half, even, odd)
        den = pltpu.roll(jnp.where(low_half, odd, even), V_HEAD_DIM, 1)
        o_ref[:, hp * LANES:(hp + 1) * LANES] = (num / den).astype(BF16)


def _mla_attn(q, k, v, batch, seq):
    assert Q_TILE == KV_TILE and Q_TILE % CHUNK == 0
    return pl.pallas_call(
        _mla_attn_kernel,
        grid=(batch, seq // Q_TILE),
        in_specs=[pl.BlockSpec((Q_TILE, QK_PAD), lambda b, i: (b * (seq // Q_TILE) + i, 0)),
                  pl.BlockSpec((seq, QK_PAD), lambda b, i: (b, 0)),
                  pl.BlockSpec((seq, QK_PAD), lambda b, i: (b, 0))],
        out_specs=pl.BlockSpec((Q_TILE, MLA_WIDTH), lambda b, i: (b * (seq // Q_TILE) + i, 0)),
        out_shape=jax.ShapeDtypeStruct((batch * seq, MLA_WIDTH), BF16),
        scratch_shapes=[pltpu.VMEM((MLA_HEADS, Q_TILE, HEAD_PAD), F32)] * 2,
        compiler_params=_params(("arbitrary", "arbitrary")),
        name="mla_attn",
    )(q, k, v)


def _xa_kv_kernel(mem_ref, wk_ref, wv_ref, k_ref, v_ref):
    m = mem_ref[...].astype(BF16)
    k_ref[...] = _dot(m, wk_ref[...]).astype(BF16)
    v_ref[...] = _dot(m, wv_ref[...]).astype(BF16)


def _xa_kv(mem2d, wk, wv):
    t = mem2d.shape[0]
    tm = ROW_TILE
    row = pl.BlockSpec((tm, D_MODEL), lambda i: (i, 0))
    return pl.pallas_call(
        _xa_kv_kernel,
        grid=(t // tm,),
        in_specs=[row, _const_spec(wk.shape), _const_spec(wv.shape)],
        out_specs=[row, row],
        out_shape=[jax.ShapeDtypeStruct((t, D_MODEL), BF16)] * 2,
        compiler_params=_params(("arbitrary",)),
        name="xa_kv",
    )(mem2d, wk, wv)


def _conv_pieces(slot, first_in_seq, never, hg_ref, halo_ref, dww_ref, dwb_ref, cng_ref, cnb_ref, yc_ref, hs_ref):
    rows = hg_ref.shape[0]
    half = rows // 2
    base = CONV_HALO - (CONV_TAPS - 1)
    n_shift = half + CONV_HALO - SUBLANES

    def not_before(v, start):
        if start is None:
            return v
        token = jnp.concatenate([start] * (v.shape[1] // LANES), axis=1)
        return jnp.concatenate([jnp.where(never, token, v[0:SUBLANES]), v[SUBLANES:]], axis=0)

    def fill(r_half, start):
        if r_half == 0:
            halo = jnp.where(first_in_seq, 0.0, halo_ref[...])
        else:
            halo = hg_ref[r_half - CONV_HALO:r_half, :]
        hs_ref[0, 0:CONV_HALO, :] = not_before(halo, start)
        hs_ref[0, CONV_HALO:CONV_HALO + half, :] = hg_ref[r_half:r_half + half, :]
        for s in range(1, SUBLANES):
            hs_ref[s, 0:n_shift, :] = hs_ref[0, s:s + n_shift, :]

    def chunk(r_half, r0, start):
        acc = jnp.zeros((CONV_ROWS, CONV_WIDTH), F32) + dwb_ref[...]
        for t in range(CONV_TAPS):
            s, a = (base + t) % SUBLANES, (base + t) // SUBLANES * SUBLANES
            w_t = jnp.concatenate([dww_ref[t]] * (CONV_ROWS // SUBLANES), axis=0)
            acc = acc + hs_ref[s, r0 + a:r0 + a + CONV_ROWS, :] * w_t
            if t == 0:
                acc = not_before(acc, start)
        y = _layer_norm(acc, cng_ref[...], cnb_ref[...])
        y = y * _sigmoid(y)
        yc_ref[slot, r_half + r0:r_half + r0 + CONV_ROWS, :] = y.astype(BF16)
        return y[0:SUBLANES, 0:LANES]

    pieces = []
    for r_half in (0, half):
        pieces.append(functools.partial(fill, r_half))
        pieces += [functools.partial(chunk, r_half, r0) for r0 in range(0, half, CONV_ROWS)]
    return pieces


def _tail_kernel(tiles_per_seq, n_tiles, o_ref, x_ref, hg_ref, halo_ref, kx_ref, vx_ref,
                 woa_ref, woc_ref, g1_ref, b1_ref, wq_ref, wo_ref, g2_ref, b2_ref,
                 wg_ref, wu_ref, wd_ref, g3_ref, b3_ref, dww_ref, dwb_ref, cng_ref, cnb_ref,
                 out_ref, yc_ref, hs_ref, qx_ref, ox_ref):
    step = pl.program_id(0)
    conv_tile = jnp.minimum(step, n_tiles - 1)

    @pl.when(step == 0)
    def _():
        yc_ref[...] = jnp.zeros(yc_ref.shape, BF16)

    never = step < 0
    pieces = iter(_conv_pieces(step % 2, conv_tile % tiles_per_seq == 0, never, hg_ref, halo_ref, dww_ref,
                               dwb_ref, cng_ref, cnb_ref, yc_ref, hs_ref))

    anchor = [None]

    def emit_conv(n, start=None):
        for piece in (pieces if n is None else itertools.islice(pieces, n)):
            token = piece(start)
            if token is not None:
                anchor[0] = token if anchor[0] is None else jnp.where(never, anchor[0], token)

    def anchored(v):
        if anchor[0] is None:
            return v
        token = jnp.concatenate([anchor[0]] * (v.shape[1] // LANES), axis=1)
        anchor[0] = None
        return jnp.concatenate([jnp.where(never, token, v[0:SUBLANES]), v[SUBLANES:]], axis=0)

    yc_prev = yc_ref.at[(step + 1) % 2]
    half = x_ref.shape[0] // 2
    halves = (slice(0, half), slice(half, 2 * half))

    ys = [_dot(o_ref[r, :], woa_ref[...]) + _dot(yc_prev[r, :], woc_ref[...]) for r in halves]
    emit_conv(CONV_PIECES_BEFORE_SWIGLU[0])
    for r, y in zip(halves, ys):
        x1 = _layer_norm(ALPHA * x_ref[r, :] + y, g1_ref[...], b1_ref[...])
        out_ref[r, :] = x1
        qx_ref[r, :] = (_dot(x1.astype(BF16), wq_ref[...]) * (XA_HEAD_DIM ** -0.5 * LOG2_E)).astype(BF16)
    emit_conv(CONV_PIECES_BEFORE_SWIGLU[1])

    def head_lanes(h):
        return slice(h * XA_HEAD_DIM, (h + 1) * XA_HEAD_DIM)

    def scores(unit):
        r, h = unit
        return _dot_nt(qx_ref[r, head_lanes(h)], kx_ref[:, head_lanes(h)])

    units = [(r, h) for h in range(XA_HEADS) for r in halves]
    s_next = scores(units[0])
    for i, (r, h) in enumerate(units):
        s = s_next
        if i + 1 < len(units):
            s_next = scores(units[i + 1])
        p = jnp.exp2(s - jnp.max(s, axis=1, keepdims=True))
        l = jnp.sum(p, axis=1, keepdims=True)
        ox_ref[r, head_lanes(h)] = (_dot(p.astype(BF16), vx_ref[:, head_lanes(h)]) / l).astype(BF16)
    emit_conv(CONV_PIECES_BEFORE_SWIGLU[2])

    ys = [_dot(ox_ref[r, :], wo_ref[...]) for r in halves]
    emit_conv(CONV_PIECES_BEFORE_SWIGLU[3])
    for r, y in zip(halves, ys):
        out_ref[r, :] = _layer_norm(ALPHA * out_ref[r, :] + y, g2_ref[...], b2_ref[...])

    x2 = out_ref[...]
    xb = x2.astype(BF16)
    hid_refs = (qx_ref, ox_ref)
    dots_issued = [0]

    def after_dot(result):
        dots_issued[0] += 1
        if dots_issued[0] % FFN_DOTS_PER_CONV_PIECE == 0:
            emit_conv(1, result[0:SUBLANES, 0:LANES])

    def gate_up(c):
        c0, cw = FFN_CHUNKS[c]
        for j in range(0, cw, MXU_COLS):
            cols = slice(c0 + j, c0 + j + MXU_COLS)
            gate = _dot(xb, wg_ref[:, cols])
            after_dot(gate)
            up = _dot(xb, wu_ref[:, cols])
            after_dot(up)
            hid_refs[c % 2][:, j:j + MXU_COLS] = anchored(gate * _sigmoid(gate) * up).astype(BF16)

    ys = [None] * (D_MODEL // MXU_COLS)

    def down(c):
        c0, cw = FFN_CHUNKS[c]
        hid = hid_refs[c % 2][:, 0:cw]
        for n in range(len(ys)):
            part = _dot(hid, wd_ref[c0:c0 + cw, n * MXU_COLS:(n + 1) * MXU_COLS])
            after_dot(part)
            ys[n] = part if ys[n] is None else ys[n] + part

    gate_up(0)
    for c in range(len(FFN_CHUNKS)):
        if c + 1 < len(FFN_CHUNKS):
            gate_up(c + 1)
        down(c)
    emit_conv(None)
    out_ref[...] = _layer_norm(ALPHA * x2 + jnp.concatenate(ys, axis=1), g3_ref[...], b3_ref[...])


def _tail(o, x2d, hg, kx, vx, weights, seq, mem_len):
    t = x2d.shape[0]
    tm = ROW_TILE
    n_tiles = t // tm
    halo_per_tile = tm // CONV_HALO
    prev = lambda s: jnp.maximum(s - 1, 0)
    cur = lambda s: jnp.minimum(s, n_tiles - 1)
    row_prev = lambda w: pl.BlockSpec((tm, w), lambda s: (prev(s), 0))
    in_specs = [row_prev(MLA_WIDTH), row_prev(D_MODEL),
                pl.BlockSpec((tm, CONV_WIDTH), lambda s: (cur(s), 0)),
                pl.BlockSpec((CONV_HALO, CONV_WIDTH), lambda s: (jnp.maximum(cur(s) * halo_per_tile - 1, 0), 0)),
                pl.BlockSpec((mem_len, D_MODEL), lambda s: (prev(s) // (seq // tm), 0)),
                pl.BlockSpec((mem_len, D_MODEL), lambda s: (prev(s) // (seq // tm), 0))]
    return pl.pallas_call(
        functools.partial(_tail_kernel, seq // tm, n_tiles),
        grid=(n_tiles + 1,),
        in_specs=in_specs + [_const_spec(c.shape) for c in weights],
        out_specs=row_prev(D_MODEL),
        out_shape=jax.ShapeDtypeStruct((t, D_MODEL), F32),
        scratch_shapes=[pltpu.VMEM((2, tm, CONV_WIDTH), BF16),
                        pltpu.VMEM((SUBLANES, tm // 2 + CONV_HALO, CONV_WIDTH), F32),
                        pltpu.VMEM((tm, D_MODEL), BF16), pltpu.VMEM((tm, D_MODEL), BF16)],
        compiler_params=_params(("arbitrary",)),
        name="tail",
    )(o, x2d, hg, hg, kx, vx, *weights)


def _pad_head(parts):
    w = jnp.concatenate(parts, axis=-1)
    pad = HEAD_PAD - w.shape[-1]
    return jnp.pad(w, [(0, 0)] * (w.ndim - 1) + [(0, pad)])


def _rope_pair(w):
    x1, x2 = w[..., :HALF_ROPE], w[..., HALF_ROPE:]
    z = jnp.zeros(w.shape[:-1] + (QK_NOPE_DIM,), w.dtype)
    return _pad_head([z, x1, x2]), _pad_head([z, -x2, x1])


def _layer_weights(l, mix_w_in, mix_b_in, mla_q_norm, mla_w_uq, mla_kv_norm, mla_w_ukv, conv_dw_w,
                   conv_dw_b, conv_norm_g, conv_norm_b, mix_w_o, ln1_g, ln1_b, xa_w_q, xa_w_kv, xa_w_o,
                   ln2_g, ln2_b, ffn_w_in, ffn_w_down, ln3_g, ln3_b):
    row = lambda v: v[l][None, :].astype(F32)
    o1 = Q_LORA_RANK
    o2 = o1 + KV_LORA_RANK
    o3 = o2 + QK_ROPE_DIM
    w_in, b_in = mix_w_in[l], mix_b_in[l][None, :]
    kra_w, krb_w = _rope_pair(w_in[:, o2:o3])
    kra_b, krb_b = _rope_pair(b_in[:, o2:o3])
    w_in_ext = jnp.concatenate([w_in[:, :o2], kra_w, krb_w, w_in[:, o3:]], axis=1).astype(BF16)
    b_in_ext = jnp.concatenate([b_in[:, :o2], kra_b, krb_b, b_in[:, o3:]], axis=1).astype(F32)

    w_uq = mla_w_uq[l].reshape(Q_LORA_RANK, MLA_HEADS, QK_NOPE_DIM + QK_ROPE_DIM)
    q_nope, q_rope = w_uq[..., :QK_NOPE_DIM], w_uq[..., QK_NOPE_DIM:]
    qr_a, qr_b = _rope_pair(q_rope)
    wqa = (_pad_head([q_nope]) + qr_a).reshape(Q_LORA_RANK, QK_PAD).astype(BF16)
    wqb = qr_b.reshape(Q_LORA_RANK, QK_PAD).astype(BF16)

    w_ukv = mla_w_ukv[l].reshape(KV_LORA_RANK, MLA_HEADS, QK_NOPE_DIM + V_HEAD_DIM)
    wk = _pad_head([w_ukv[..., :QK_NOPE_DIM]]).reshape(KV_LORA_RANK, QK_PAD).astype(BF16)
    w_uv = w_ukv[..., QK_NOPE_DIM:]
    zv = jnp.zeros_like(w_uv)
    wv = jnp.where((jnp.arange(MLA_HEADS) % 2 == 0)[None, :, None],
                   jnp.concatenate([w_uv, zv], axis=-1), jnp.concatenate([zv, w_uv], axis=-1))
    wv = wv.reshape(KV_LORA_RANK, QK_PAD).astype(BF16)

    return dict(
        mix_in=(w_in_ext, b_in_ext, row(mla_q_norm), row(mla_kv_norm), wqa, wqb, wk, wv),
        xa_kv=(xa_w_kv[l][:, :D_MODEL].astype(BF16), xa_w_kv[l][:, D_MODEL:].astype(BF16)),
        tail=(mix_w_o[l][:MLA_WIDTH].astype(BF16), mix_w_o[l][MLA_WIDTH:].astype(BF16), row(ln1_g), row(ln1_b),
              xa_w_q[l].astype(BF16), xa_w_o[l].astype(BF16), row(ln2_g), row(ln2_b),
              ffn_w_in[l][:, :FFN_HIDDEN].astype(BF16), ffn_w_in[l][:, FFN_HIDDEN:].astype(BF16),
              ffn_w_down[l].astype(BF16), row(ln3_g), row(ln3_b),
              jnp.broadcast_to(conv_dw_w[l].astype(F32)[:, None, :], (CONV_TAPS, SUBLANES, CONV_WIDTH)),
              row(conv_dw_b), row(conv_norm_g), row(conv_norm_b)),
    )


def _v_ones():
    vone = np.zeros((1, MLA_HEADS, HEAD_PAD), np.float32)
    vone[:, 0::2, V_HEAD_DIM:] = 1.0
    vone[:, 1::2, :V_HEAD_DIM] = 1.0
    return jnp.asarray(vone.reshape(1, QK_PAD))


def kernel(x, mem, positions, mix_w_in, mix_b_in, mla_q_norm, mla_w_uq, mla_kv_norm, mla_w_ukv, conv_dw_w,
           conv_dw_b, conv_norm_g, conv_norm_b, mix_w_o, ln1_g, ln1_b, xa_w_q, xa_w_kv, xa_w_o, ln2_g, ln2_b,
           ffn_w_in, ffn_w_down, ln3_g, ln3_b):
    batch, seq, d = x.shape
    mem_len = mem.shape[1]
    assert d == D_MODEL and seq % ROW_TILE == 0 and seq % Q_TILE == 0 and ffn_w_down.shape[1] == FFN_HIDDEN
    assert (batch * mem_len) % ROW_TILE == 0 and ROW_TILE % (2 * CONV_ROWS) == 0 and ROW_TILE // 2 >= CONV_HALO
    weights = (mix_w_in, mix_b_in, mla_q_norm, mla_w_uq, mla_kv_norm, mla_w_ukv, conv_dw_w, conv_dw_b,
               conv_norm_g, conv_norm_b, mix_w_o, ln1_g, ln1_b, xa_w_q, xa_w_kv, xa_w_o, ln2_g, ln2_b,
               ffn_w_in, ffn_w_down, ln3_g, ln3_b)
    vone = _v_ones()
    cos, sin = _rope_tables(positions)
    h = x.reshape(batch * seq, d)
    mem2d = mem.reshape(batch * mem_len, d)
    for l in range(DEPTH):
        w = _layer_weights(l, *weights)
        q, k, v, hg = _mix_in(h, cos, sin, *w["mix_in"], vone)
        o = _mla_attn(q, k, v, batch, seq)
        kx, vx = _xa_kv(mem2d, *w["xa_kv"])
        h = _tail(o, h, hg, kx, vx, w["tail"], seq, mem_len)
    return h.reshape(batch, seq, d)
```
